```python
import math
import jax, jax.numpy as jnp
from jax import lax
import numpy as np

D_MODEL = 1024
BATCH = 32
SEQ = 2048
DEPTH = 1

D_SSM = D_MODEL // 2
SSM_GROUP = 16
N_SSM_GROUPS = D_SSM // SSM_GROUP
SSM_STATE = 64
D_CONV = D_MODEL // 2
CONV_WIDTH = 3
D_FF = 4 * D_MODEL
N_BRANCH = 2
N_MOD = 6
RMS_EPS = 1e-6
DT_MIN = 1e-3
DT_MAX = 1e-1
IN_COLS = D_SSM + 3 * D_CONV + N_BRANCH * D_MODEL

kernel_name = "hybrid_s5_shortconv_gated_adaln_block"


def rmsnorm(x, g):
    xf = x.astype(jnp.float32)
    y = xf * lax.rsqrt(jnp.mean(xf * xf, axis=-1, keepdims=True) + RMS_EPS)
    return (y * g.astype(jnp.float32)).astype(x.dtype)


def modulate(h, shift, scale):
    return h * (1 + scale[:, None, :]) + shift[:, None, :]


def s5_mimo(u, lam_re, lam_im, log_dt, b_re, b_im, c_re, c_im, d_skip):
    f32 = jnp.float32
    n_len = u.shape[1]
    u32 = u.astype(f32)
    lam = lax.complex(lam_re.astype(f32), lam_im.astype(f32))
    dt = jnp.exp(log_dt.astype(f32))[:, None]
    lam_bar = jnp.exp(lam * dt)
    b = lax.complex(b_re.astype(f32), b_im.astype(f32))
    b_bar = ((lam_bar - 1) / lam)[..., None] * b
    bu = jnp.einsum('blgh,gph->blgp', u32.astype(jnp.complex64), b_bar)
    a = jnp.broadcast_to(lam_bar[None, None], (1, n_len) + lam_bar.shape)

    def combine(e1, e2):
        a1, s1 = e1
        a2, s2 = e2
        return a1 * a2, a2 * s1 + s2

    _, states = lax.associative_scan(combine, (a, bu), axis=1)
    c = lax.complex(c_re.astype(f32), c_im.astype(f32))
    y = jnp.real(jnp.einsum('blgp,ghp->blgh', states, c))
    y = y + d_skip.astype(f32) * u32
    return y.astype(u.dtype)


def short_gated_conv(cx, cb, cc, conv_w):
    v = cc * cx
    w = conv_w.astype(v.dtype)[:, None, :]
    y = lax.conv_general_dilated(
        v, w, window_strides=(1,), padding=[(CONV_WIDTH - 1, 0)],
        dimension_numbers=('NWC', 'WIO', 'NWC'), feature_group_count=D_CONV)
    return cb * y


def setup_inputs(seed: int = 0) -> dict:
    key = jax.random.key(seed)
    ks = jax.random.split(key, 24)
    f32 = jnp.float32
    G, H, P = N_SSM_GROUPS, SSM_GROUP, SSM_STATE

    def nrm(k, shape, scale):
        return jax.random.normal(k, shape, f32) * scale

    x = jax.random.normal(ks[0], (BATCH, SEQ, D_MODEL), f32)
    c = jax.random.normal(ks[1], (BATCH, D_MODEL), f32)
    norm1_g = 1.0 + nrm(ks[2], (DEPTH, D_MODEL), 0.02)
    norm2_g = 1.0 + nrm(ks[3], (DEPTH, D_MODEL), 0.02)
    w_ada = nrm(ks[4], (DEPTH, D_MODEL, N_MOD * D_MODEL), 0.5 * D_MODEL ** -0.5)
    b_ada = nrm(ks[5], (DEPTH, N_MOD * D_MODEL), 0.01)
    w_in = nrm(ks[6], (DEPTH, D_MODEL, IN_COLS), D_MODEL ** -0.5)
    lam_re = -0.5 + nrm(ks[7], (DEPTH, G, P), 0.02)
    lam_im = math.pi * jnp.arange(P, dtype=f32)[None, None, :] + nrm(ks[8], (DEPTH, G, P), 0.02)
    log_dt = jax.random.uniform(ks[9], (DEPTH, G), f32, math.log(DT_MIN), math.log(DT_MAX))
    b_re = nrm(ks[10], (DEPTH, G, P, H), (2 * H) ** -0.5)
    b_im = nrm(ks[11], (DEPTH, G, P, H), (2 * H) ** -0.5)
    c_re = nrm(ks[12], (DEPTH, G, H, P), P ** -0.5)
    c_im = nrm(ks[13], (DEPTH, G, H, P), P ** -0.5)
    d_skip = nrm(ks[14], (DEPTH, D_SSM), 1.0)
    w_glu = nrm(ks[15], (DEPTH, D_SSM, D_SSM), D_SSM ** -0.5)
    b_glu = nrm(ks[16], (DEPTH, D_SSM), 0.01)
    conv_w = nrm(ks[17], (DEPTH, CONV_WIDTH, D_CONV), CONV_WIDTH ** -0.5)
    w_proj_ssm = nrm(ks[18], (DEPTH, D_SSM, D_MODEL), D_SSM ** -0.5)
    w_proj_conv = nrm(ks[19], (DEPTH, D_CONV, D_MODEL), D_CONV ** -0.5)
    w_out = nrm(ks[20], (DEPTH, D_MODEL, D_MODEL), D_MODEL ** -0.5)
    w_ff1 = nrm(ks[21], (DEPTH, D_MODEL, D_FF), D_MODEL ** -0.5)
    w_ff2 = nrm(ks[22], (DEPTH, D_FF, D_MODEL), D_FF ** -0.5)
    final_g = 1.0 + nrm(ks[23], (D_MODEL,), 0.02)
    return {"x": x, "c": c, "norm1_g": norm1_g, "norm2_g": norm2_g,
            "w_ada": w_ada, "b_ada": b_ada, "w_in": w_in,
            "lam_re": lam_re, "lam_im": lam_im, "log_dt": log_dt,
            "b_re": b_re, "b_im": b_im, "c_re": c_re, "c_im": c_im,
            "d_skip": d_skip, "w_glu": w_glu, "b_glu": b_glu, "conv_w": conv_w,
            "w_proj_ssm": w_proj_ssm, "w_proj_conv": w_proj_conv, "w_out": w_out,
            "w_ff1": w_ff1, "w_ff2": w_ff2, "final_g": final_g}


def reference(x, c, norm1_g, norm2_g, w_ada, b_ada, w_in, lam_re, lam_im, log_dt,
              b_re, b_im, c_re, c_im, d_skip, w_glu, b_glu, conv_w,
              w_proj_ssm, w_proj_conv, w_out, w_ff1, w_ff2, final_g):
    n_b, n_len, _ = x.shape
    split_at = [D_SSM, D_SSM + D_CONV, D_SSM + 2 * D_CONV, D_SSM + 3 * D_CONV,
                D_SSM + 3 * D_CONV + D_MODEL]
    c_act = jax.nn.silu(c)
    for l in range(DEPTH):
        mod = c_act @ w_ada[l] + b_ada[l]
        sh1, sc1, g1, sh2, sc2, g2 = jnp.split(mod, N_MOD, axis=-1)

        h = modulate(rmsnorm(x, norm1_g[l]), sh1, sc1)
        p = h @ w_in[l]
        u_s, cb, cc, cx, gate_s, gate_c = jnp.split(p, split_at, axis=-1)

        u_g = u_s.reshape(n_b, n_len, N_SSM_GROUPS, SSM_GROUP)
        y_s = s5_mimo(u_g, lam_re[l], lam_im[l], log_dt[l], b_re[l], b_im[l],
                      c_re[l], c_im[l], d_skip[l].reshape(N_SSM_GROUPS, SSM_GROUP))
        y_s = jax.nn.gelu(y_s.reshape(n_b, n_len, D_SSM))
        y_s = y_s * jax.nn.sigmoid(y_s @ w_glu[l] + b_glu[l])

        y_c = short_gated_conv(cx, cb, cc, conv_w[l])

        merged = (jax.nn.sigmoid(gate_s) * (y_s @ w_proj_ssm[l])
                  + jax.nn.sigmoid(gate_c) * (y_c @ w_proj_conv[l]))
        x = x + g1[:, None, :] * (merged @ w_out[l])

        h2 = modulate(rmsnorm(x, norm2_g[l]), sh2, sc2)
        f = jnp.square(jax.nn.relu(h2 @ w_ff1[l])) @ w_ff2[l]
        x = x + g2[:, None, :] * f
    return rmsnorm(x, final_g)
```

```python
import functools
import math

import jax
import jax.numpy as jnp
from jax import lax
from jax.experimental import pallas as pl
from jax.experimental.pallas import tpu as pltpu

SUBLANES = 8
LANES = 128
VMEM_BYTES_V7X = 64 * 1024 * 1024

RMS_EPS = 1e-6
N_MOD = 6
SSM_GROUP = 16
SSM_STATE = 64
CONV_WIDTH = 3

F32 = jnp.float32
BF16 = jnp.bfloat16


def _tiles(d_model):
    batch_tile = SUBLANES
    time_tile = 64
    mlp_rows = 512
    ff_chunk = 1024
    scan_cols = 8
    mixer_vmem = VMEM_BYTES_V7X - 8 * 1024 * 1024
    mlp_vmem = VMEM_BYTES_V7X - 16 * 1024 * 1024
    return batch_tile, time_tile, mlp_rows, ff_chunk, scan_cols, mixer_vmem, mlp_vmem


def _rms_normalize(x):
    return x * lax.rsqrt(jnp.mean(x * x, axis=-1, keepdims=True) + RMS_EPS)


def _dot(a, b):
    return jnp.dot(a, b, preferred_element_type=F32)


def _adaln_kernel(c_ref, w_ref, b_ref, o_ref):
    c = c_ref[...]
    c_act = c * jax.nn.sigmoid(c)
    mod = jnp.dot(c_act, w_ref[...], preferred_element_type=F32,
                  precision=lax.Precision.HIGHEST)
    o_ref[0] = mod + b_ref[...]


def _adaln(c, w_ada, b_ada):
    n_b, d = c.shape
    return pl.pallas_call(
        _adaln_kernel,
        grid=(N_MOD,),
        in_specs=[
            pl.BlockSpec((n_b, d), lambda j: (0, 0)),
            pl.BlockSpec((d, d), lambda j: (0, j)),
            pl.BlockSpec((1, d), lambda j: (0, j)),
        ],
        out_specs=pl.BlockSpec((1, n_b, d), lambda j: (j, 0, 0)),
        out_shape=jax.ShapeDtypeStruct((N_MOD, n_b, d), F32),
        name="adaln",
    )(c, w_ada, b_ada.reshape(1, N_MOD * d))


def _mixer_kernel(x_ref, mod_ref, g_ref, w_in_ref, bc_ref, cc_ref, lam_ref, dskip_ref,
                  wglu_ref, bglu_ref, convw_ref, wps_ref, wpc_ref, wout_ref,
                  o_ref,
                  h_ref, u_ref, utb_ref, bu_ref, ytb_ref, ybt_ref, state_ref, vpad_ref,
                  *, time_tile, scan_cols, d_ssm, d_conv, d_model):
    t_n = time_tile
    n_bt = x_ref.shape[0]
    n_slab = d_ssm // LANES
    half_cols = bu_ref.shape[1] // 2
    n_pair = bu_ref.shape[1] // (2 * LANES)

    @pl.when(pl.program_id(1) == 0)
    def _():
        state_ref[...] = jnp.zeros_like(state_ref)
        vpad_ref[:, 0:SUBLANES, :] = jnp.zeros((n_bt, SUBLANES, d_conv), F32)

    for b in range(n_bt):
        xb = x_ref[b]
        hb = _rms_normalize(xb) * g_ref[...]
        hb = hb * (1.0 + mod_ref[1, b]) + mod_ref[0, b]
        h_ref[b * t_n:(b + 1) * t_n, :] = hb.astype(BF16)
    h = h_ref[...]

    u_ref[...] = _dot(h, w_in_ref[:, 0:d_ssm])
    for b in range(n_bt):
        for j in range(n_slab):
            utb_ref[j, pl.ds(b, t_n, stride=n_bt), :] = (
                u_ref[b * t_n:(b + 1) * t_n, j * LANES:(j + 1) * LANES])
    slabs_per_half = n_slab // 2
    for half in range(2):
        lhs = jnp.concatenate(
            [utb_ref[half * slabs_per_half + j] for j in range(slabs_per_half)], axis=1)
        bu_ref[:, half * half_cols:(half + 1) * half_cols] = _dot(lhs.astype(BF16), bc_ref[half])

    for p0 in range(0, n_pair, scan_cols):
        lam_re = [lam_ref[0, :, (p0 + i) * LANES:(p0 + i + 1) * LANES] for i in range(scan_cols)]
        lam_im = [lam_ref[1, :, (p0 + i) * LANES:(p0 + i + 1) * LANES] for i in range(scan_cols)]
        init = []
        for i in range(scan_cols):
            init.append(state_ref[0, :, (p0 + i) * LANES:(p0 + i + 1) * LANES])
            init.append(state_ref[1, :, (p0 + i) * LANES:(p0 + i + 1) * LANES])

        def step(t, carry, p0=p0, lam_re=lam_re, lam_im=lam_im):
            rows = pl.ds(pl.multiple_of(t * n_bt, SUBLANES), n_bt)
            new = []
            for i in range(scan_cols):
                c_re = (p0 + i) * 2 * LANES
                c_im = c_re + LANES
                x_re, x_im = carry[2 * i], carry[2 * i + 1]
                n_re = lam_re[i] * x_re - lam_im[i] * x_im + bu_ref[rows, c_re:c_re + LANES]
                n_im = lam_re[i] * x_im + lam_im[i] * x_re + bu_ref[rows, c_im:c_im + LANES]
                bu_ref[rows, c_re:c_re + LANES] = n_re
                bu_ref[rows, c_im:c_im + LANES] = n_im
                new += [n_re, n_im]
            return tuple(new)

        final = lax.fori_loop(0, t_n, step, tuple(init), unroll=2)
        for i in range(scan_cols):
            state_ref[0, :, (p0 + i) * LANES:(p0 + i + 1) * LANES] = final[2 * i]
            state_ref[1, :, (p0 + i) * LANES:(p0 + i + 1) * LANES] = final[2 * i + 1]

    for half in range(2):
        xs = bu_ref[:, half * half_cols:(half + 1) * half_cols].astype(BF16)
        yh = _dot(xs, cc_ref[half])
        for j in range(slabs_per_half):
            ytb_ref[half * slabs_per_half + j] = yh[:, j * LANES:(j + 1) * LANES]
    for b in range(n_bt):
        for j in range(n_slab):
            ybt_ref[b * t_n:(b + 1) * t_n, j * LANES:(j + 1) * LANES] = (
                ytb_ref[j, pl.ds(b, t_n, stride=n_bt), :])

    ys = ybt_ref[...] + dskip_ref[...] * u_ref[...]
    ys = jax.nn.gelu(ys)
    ys = ys * jax.nn.sigmoid(_dot(ys.astype(BF16), wglu_ref[...]) + bglu_ref[...])

    pc = _dot(h, w_in_ref[:, d_ssm:d_ssm + 3 * d_conv])
    cb = pc[:, 0:d_conv]
    v = pc[:, d_conv:2 * d_conv] * pc[:, 2 * d_conv:3 * d_conv]
    v3 = v.reshape(n_bt, t_n, d_conv)
    vpad_ref[:, SUBLANES:SUBLANES + t_n, :] = v3
    conv = convw_ref[CONV_WIDTH - 1:CONV_WIDTH, :] * v3
    for k in range(1, CONV_WIDTH):
        shifted = vpad_ref[:, SUBLANES - k:SUBLANES - k + t_n, :]
        conv = conv + convw_ref[CONV_WIDTH - 1 - k:CONV_WIDTH - k, :] * shifted
    vpad_ref[:, 0:SUBLANES, :] = vpad_ref[:, t_n:t_n + SUBLANES, :]
    yc = cb * conv.reshape(n_bt * t_n, d_conv)

    c0 = d_ssm + 3 * d_conv
    merged = jax.nn.sigmoid(_dot(h, w_in_ref[:, c0:c0 + d_model])) * _dot(ys.astype(BF16), wps_ref[...])
    merged = merged + (jax.nn.sigmoid(_dot(h, w_in_ref[:, c0 + d_model:c0 + 2 * d_model]))
                       * _dot(yc.astype(BF16), wpc_ref[...]))
    mixed = _dot(merged.astype(BF16), wout_ref[...])
    for b in range(n_bt):
        o_ref[b] = x_ref[b] + mod_ref[2, b] * mixed[b * t_n:(b + 1) * t_n, :]


def _resident(shape):
    zeros = (0,) * len(shape)
    return pl.BlockSpec(shape, lambda *_: zeros, pipeline_mode=pl.Buffered(1))


def _mixer(x, mod, norm_g, w_in, bc, cc, lam, d_skip, w_glu, b_glu, conv_w, w_ps, w_pc, w_out):
    n_b, n_len, d = x.shape
    d_ssm = w_glu.shape[0]
    d_conv = conv_w.shape[1]
    n_state_cols = lam.shape[2] * 2
    n_bt, t_n, _, _, scan_cols, vmem, _ = _tiles(d)
    m = n_bt * t_n
    kern = functools.partial(_mixer_kernel, time_tile=t_n, scan_cols=scan_cols,
                             d_ssm=d_ssm, d_conv=d_conv, d_model=d)
    return pl.pallas_call(
        kern,
        grid=(n_b // n_bt, n_len // t_n),
        in_specs=[
            pl.BlockSpec((n_bt, t_n, d), lambda i, t: (i, t, 0)),
            pl.BlockSpec((3, n_bt, 1, d), lambda i, t: (0, i, 0, 0)),
            _resident(norm_g.shape), _resident(w_in.shape), _resident(bc.shape),
            _resident(cc.shape), _resident(lam.shape), _resident(d_skip.shape),
            _resident(w_glu.shape), _resident(b_glu.shape), _resident(conv_w.shape),
            _resident(w_ps.shape), _resident(w_pc.shape), _resident(w_out.shape),
        ],
        out_specs=pl.BlockSpec((n_bt, t_n, d), lambda i, t: (i, t, 0)),
        out_shape=jax.ShapeDtypeStruct(x.shape, F32),
        scratch_shapes=[
            pltpu.VMEM((m, d), BF16),
            pltpu.VMEM((m, d_ssm), F32),
            pltpu.VMEM((d_ssm // LANES, m, LANES), F32),
            pltpu.VMEM((m, n_state_cols), F32),
            pltpu.VMEM((d_ssm // LANES, m, LANES), F32),
            pltpu.VMEM((m, d_ssm), F32),
            pltpu.VMEM((2, n_bt, n_state_cols // 2), F32),
            pltpu.VMEM((n_bt, t_n + SUBLANES, d_conv), F32),
        ],
        compiler_params=pltpu.CompilerParams(
            dimension_semantics=("arbitrary", "arbitrary"), vmem_limit_bytes=vmem),
        name="mixer",
    )(x, mod, norm_g, w_in, bc, cc, lam, d_skip, w_glu, b_glu, conv_w, w_ps, w_pc, w_out)


def _mlp_kernel(x_ref, mod_ref, g_ref, w1_ref, w2_ref, fg_ref, o_ref, *, ff_chunk, final_norm):
    x = x_ref[0]
    h = _rms_normalize(x) * g_ref[...]
    h = (h * (1.0 + mod_ref[1, 0]) + mod_ref[0, 0]).astype(BF16)
    d_ff = w1_ref.shape[1]
    f = None
    for c0 in range(0, d_ff, ff_chunk):
        a = jnp.maximum(_dot(h, w1_ref[:, c0:c0 + ff_chunk]), 0.0)
        part = _dot((a * a).astype(BF16), w2_ref[c0:c0 + ff_chunk, :])
        f = part if f is None else f + part
    x2 = x + mod_ref[2, 0] * f
    o_ref[0] = _rms_normalize(x2) * fg_ref[...] if final_norm else x2


def _mlp(x, mod, norm_g, w1, w2, final_g, final_norm):
    n_b, n_len, d = x.shape
    _, _, rows, ff_chunk, _, _, vmem = _tiles(d)
    return pl.pallas_call(
        functools.partial(_mlp_kernel, ff_chunk=ff_chunk, final_norm=final_norm),
        grid=(n_b, n_len // rows),
        in_specs=[
            pl.BlockSpec((1, rows, d), lambda b, i: (b, i, 0)),
            pl.BlockSpec((3, 1, 1, d), lambda b, i: (1, b, 0, 0)),
            _resident(norm_g.shape), _resident(w1.shape), _resident(w2.shape),
            _resident(final_g.shape),
        ],
        out_specs=pl.BlockSpec((1, rows, d), lambda b, i: (b, i, 0)),
        out_shape=jax.ShapeDtypeStruct(x.shape, F32),
        compiler_params=pltpu.CompilerParams(
            dimension_semantics=("arbitrary", "arbitrary"), vmem_limit_bytes=vmem),
        name="mlp",
    )(x, mod, norm_g, w1, w2, final_g)


def _s5_operators(lam_re, lam_im, log_dt, b_re, b_im, c_re, c_im):
    n_g, n_p = lam_re.shape
    n_h = b_re.shape[-1]
    dt = jnp.exp(log_dt)[:, None]
    mag = jnp.exp(lam_re * dt)
    lb_re = mag * jnp.cos(lam_im * dt)
    lb_im = mag * jnp.sin(lam_im * dt)
    den = lam_re * lam_re + lam_im * lam_im
    k_re = ((lb_re - 1.0) * lam_re + lb_im * lam_im) / den
    k_im = (lb_im * lam_re - (lb_re - 1.0) * lam_im) / den
    bb_re = k_re[..., None] * b_re - k_im[..., None] * b_im
    bb_im = k_re[..., None] * b_im + k_im[..., None] * b_re

    g_half = n_g // 2
    n_pairs = g_half // 2
    eye = jnp.eye(g_half, dtype=F32)

    def b_half(half):
        sl = slice(half * g_half, (half + 1) * g_half)
        blk = jnp.stack([jnp.swapaxes(bb_re[sl], 1, 2), jnp.swapaxes(bb_im[sl], 1, 2)], axis=2)
        full = eye[:, None, :, None, None] * blk[:, :, None, :, :]
        full = full.reshape(g_half, n_h, n_pairs, 2, 2, n_p)
        full = jnp.swapaxes(full, 3, 4)
        return full.reshape(g_half * n_h, g_half * 2 * n_p)

    def c_half(half):
        sl = slice(half * g_half, (half + 1) * g_half)
        blk = jnp.stack([c_re[sl], -c_im[sl]], axis=2)
        full = eye[:, None, :, None, None] * blk[:, :, None, :, :]
        full = full.reshape(g_half, n_h, n_pairs, 2, 2, n_p)
        full = jnp.swapaxes(full, 3, 4)
        full = full.reshape(g_half * n_h, g_half * 2 * n_p)
        return full.T

    bc = jnp.stack([b_half(0), b_half(1)]).astype(BF16)
    cc = jnp.stack([c_half(0), c_half(1)]).astype(BF16)
    lam = jnp.stack([lb_re.reshape(-1), lb_im.reshape(-1)])
    lam = jnp.broadcast_to(lam[:, None, :], (2, SUBLANES, n_g * n_p))
    return lam, bc, cc


def kernel(x, c, norm1_g, norm2_g, w_ada, b_ada, w_in, lam_re, lam_im, log_dt, b_re, b_im,
           c_re, c_im, d_skip, w_glu, b_glu, conv_w, w_proj_ssm, w_proj_conv, w_out,
           w_ff1, w_ff2, final_g):
    depth = w_in.shape[0]
    n_b, _, d = x.shape
    for l in range(depth):
        mod = _adaln(c, w_ada[l], b_ada[l]).reshape(N_MOD, n_b, 1, d)
        lam, bc, cc = _s5_operators(lam_re[l], lam_im[l], log_dt[l], b_re[l], b_im[l],
                                    c_re[l], c_im[l])
        x = _mixer(x, mod, norm1_g[l][None], w_in[l].astype(BF16), bc, cc, lam,
                   d_skip[l][None], w_glu[l].astype(BF16), b_glu[l][None], conv_w[l],
                   w_proj_ssm[l].astype(BF16), w_proj_conv[l].astype(BF16),
                   w_out[l].astype(BF16))
        x = _mlp(x, mod, norm2_g[l][None], w_ff1[l].astype(BF16), w_ff2[l].astype(BF16),
                 final_g[None], final_norm=(l == depth - 1))
    return x
```

```python
import functools
import math

import jax
import jax.numpy as jnp
from jax import lax
from jax.experimental import pallas as pl
from jax.experimental.pallas import tpu as pltpu

SUBLANES = 8
LANES = 128
MXU_COLS = 256
VMEM_BYTES_V7X = 64 * 1024 * 1024

RMS_EPS = 1e-6
N_MOD = 6
SSM_GROUP = 16
SSM_STATE = 64
CONV_WIDTH = 3

F32 = jnp.float32
BF16 = jnp.bfloat16


def _tiles(d_model):
    batch_tile = SUBLANES
    time_tile = 64
    mlp_rows = 512
    ff_chunk = 1024
    scan_cols = 8
    steps_per_slice = 4
    mixer_vmem = VMEM_BYTES_V7X - 8 * 1024 * 1024
    mlp_vmem = VMEM_BYTES_V7X - 16 * 1024 * 1024
    return (batch_tile, time_tile, mlp_rows, ff_chunk, scan_cols, steps_per_slice,
            mixer_vmem, mlp_vmem)


def _rms_normalize(x):
    return x * lax.rsqrt(jnp.mean(x * x, axis=-1, keepdims=True) + RMS_EPS)


def _dot(a, b):
    return jnp.dot(a, b, preferred_element_type=F32)


def _adaln_kernel(c_ref, w_ref, b_ref, o_ref):
    c = c_ref[...]
    c_act = c * jax.nn.sigmoid(c)
    mod = jnp.dot(c_act, w_ref[...], preferred_element_type=F32,
                  precision=lax.Precision.HIGHEST)
    o_ref[0] = mod + b_ref[...]


def _adaln(c, w_ada, b_ada):
    n_b, d = c.shape
    return pl.pallas_call(
        _adaln_kernel,
        grid=(N_MOD,),
        in_specs=[
            pl.BlockSpec((n_b, d), lambda j: (0, 0)),
            pl.BlockSpec((d, d), lambda j: (0, j)),
            pl.BlockSpec((1, d), lambda j: (0, j)),
        ],
        out_specs=pl.BlockSpec((1, n_b, d), lambda j: (j, 0, 0)),
        out_shape=jax.ShapeDtypeStruct((N_MOD, n_b, d), F32),
        name="adaln",
    )(c, w_ada, b_ada.reshape(1, N_MOD * d))


def _mixer_kernel(x_ref, mod_ref, g_ref, w_u_ref, w_rest_ref, bc_ref, cc_ref, lam_ref, dskip_ref,
                  wglu_ref, bglu_ref, convw_ref, wps_ref, wpc_ref, wout_ref,
                  o_ref,
                  h_ref, u_ref, utb_ref, bu_ref, ytb_ref, ybt_ref, state_ref, vpad_ref, p_ref,
                  *, time_tile, scan_cols, scan_steps_per_slice, d_ssm, d_conv, d_model):
    t_n = time_tile
    n_bt = x_ref.shape[0]
    n_slab = d_ssm // LANES
    half_cols = bu_ref.shape[1] // 2
    n_pair = bu_ref.shape[1] // (2 * LANES)

    @pl.when(pl.program_id(1) == 0)
    def _():
        state_ref[...] = jnp.zeros_like(state_ref)
        vpad_ref[:, 0:SUBLANES, :] = jnp.zeros((n_bt, SUBLANES, d_conv), F32)

    for b in range(n_bt):
        xb = x_ref[b]
        hb = _rms_normalize(xb) * g_ref[...]
        hb = hb * (1.0 + mod_ref[1, b]) + mod_ref[0, b]
        h_ref[b * t_n:(b + 1) * t_n, :] = hb.astype(BF16)
    h = h_ref[...]

    u_ref[...] = _dot(h, w_u_ref[...])
    for b in range(n_bt):
        for j in range(n_slab):
            utb_ref[j, pl.ds(b, t_n, stride=n_bt), :] = (
                u_ref[b * t_n:(b + 1) * t_n, j * LANES:(j + 1) * LANES])
    slabs_per_half = n_slab // 2
    for half in range(2):
        lhs = jnp.concatenate(
            [utb_ref[half * slabs_per_half + j] for j in range(slabs_per_half)], axis=1)
        bu_ref[:, half * half_cols:(half + 1) * half_cols] = _dot(lhs.astype(BF16), bc_ref[half])

    def scan_steps(t0):
        for p0 in range(0, n_pair, scan_cols):
            cols = [(p0 + i) * LANES for i in range(scan_cols)]
            lam_re = [lam_ref[0, :, c:c + LANES] for c in cols]
            lam_im = [lam_ref[1, :, c:c + LANES] for c in cols]
            x_re = [state_ref[0, :, c:c + LANES] for c in cols]
            x_im = [state_ref[1, :, c:c + LANES] for c in cols]
            for s in range(scan_steps_per_slice):
                rows = pl.ds(pl.multiple_of((t0 + s) * n_bt, SUBLANES), n_bt)
                for i, c in enumerate(cols):
                    c_re, c_im = 2 * c, 2 * c + LANES
                    n_re = lam_re[i] * x_re[i] - lam_im[i] * x_im[i] + bu_ref[rows, c_re:c_re + LANES]
                    n_im = lam_re[i] * x_im[i] + lam_im[i] * x_re[i] + bu_ref[rows, c_im:c_im + LANES]
                    bu_ref[rows, c_re:c_re + LANES] = n_re
                    bu_ref[rows, c_im:c_im + LANES] = n_im
                    x_re[i], x_im[i] = n_re, n_im
            for i, c in enumerate(cols):
                state_ref[0, :, c:c + LANES] = x_re[i]
                state_ref[1, :, c:c + LANES] = x_im[i]

    n_slice = w_rest_ref.shape[0]

    def slice_and_scan(i, carry):
        p_ref[i] = _dot(h_ref[...], w_rest_ref[i])
        scan_steps(i * scan_steps_per_slice)
        return carry

    def scan_only(i, carry):
        scan_steps(i * scan_steps_per_slice)
        return carry

    for i in range(n_slice):
        slice_and_scan(i, 0)
    for i in range(n_slice, t_n // scan_steps_per_slice):
        scan_only(i, 0)

    for half in range(2):
        xs = bu_ref[:, half * half_cols:(half + 1) * half_cols].astype(BF16)
        yh = _dot(xs, cc_ref[half])
        for j in range(slabs_per_half):
            ytb_ref[half * slabs_per_half + j] = yh[:, j * LANES:(j + 1) * LANES]
    for b in range(n_bt):
        for j in range(n_slab):
            ybt_ref[b * t_n:(b + 1) * t_n, j * LANES:(j + 1) * LANES] = (
                ytb_ref[j, pl.ds(b, t_n, stride=n_bt), :])

    ys = ybt_ref[...] + dskip_ref[...] * u_ref[...]
    ys = jax.nn.gelu(ys)
    ys = ys * jax.nn.sigmoid(_dot(ys.astype(BF16), wglu_ref[...]) + bglu_ref[...])

    def p_cols(first_col, n_cols):
        w = p_ref.shape[2]
        return jnp.concatenate([p_ref[k] for k in range(first_col // w, (first_col + n_cols) // w)],
                               axis=1)

    cb = p_cols(0, d_conv)
    v = p_cols(d_conv, d_conv) * p_cols(2 * d_conv, d_conv)
    v3 = v.reshape(n_bt, t_n, d_conv)
    vpad_ref[:, SUBLANES:SUBLANES + t_n, :] = v3
    conv = convw_ref[CONV_WIDTH - 1:CONV_WIDTH, :] * v3
    for k in range(1, CONV_WIDTH):
        shifted = vpad_ref[:, SUBLANES - k:SUBLANES - k + t_n, :]
        conv = conv + convw_ref[CONV_WIDTH - 1 - k:CONV_WIDTH - k, :] * shifted
    vpad_ref[:, 0:SUBLANES, :] = vpad_ref[:, t_n:t_n + SUBLANES, :]
    yc = cb * conv.reshape(n_bt * t_n, d_conv)

    c0 = 3 * d_conv
    merged = jax.nn.sigmoid(p_cols(c0, d_model)) * _dot(ys.astype(BF16), wps_ref[...])
    merged = merged + jax.nn.sigmoid(p_cols(c0 + d_model, d_model)) * _dot(yc.astype(BF16), wpc_ref[...])
    mixed = _dot(merged.astype(BF16), wout_ref[...])
    for b in range(n_bt):
        o_ref[b] = x_ref[b] + mod_ref[2, b] * mixed[b * t_n:(b + 1) * t_n, :]


def _resident(shape):
    zeros = (0,) * len(shape)
    return pl.BlockSpec(shape, lambda *_: zeros, pipeline_mode=pl.Buffered(1))


def _mixer(x, mod, norm_g, w_in, bc, cc, lam, d_skip, w_glu, b_glu, conv_w, w_ps, w_pc, w_out):
    n_b, n_len, d = x.shape
    d_ssm = w_glu.shape[0]
    d_conv = conv_w.shape[1]
    n_state_cols = lam.shape[2] * 2
    n_bt, t_n, _, _, scan_cols, steps_per_slice, vmem, _ = _tiles(d)
    m = n_bt * t_n
    w_u = w_in[:, :d_ssm]
    n_slice = (w_in.shape[1] - d_ssm) // MXU_COLS
    w_rest = jnp.swapaxes(w_in[:, d_ssm:].reshape(d, n_slice, MXU_COLS), 0, 1)
    assert n_slice * steps_per_slice <= t_n and t_n % steps_per_slice == 0
    kern = functools.partial(_mixer_kernel, time_tile=t_n, scan_cols=scan_cols,
                             scan_steps_per_slice=steps_per_slice,
                             d_ssm=d_ssm, d_conv=d_conv, d_model=d)
    return pl.pallas_call(
        kern,
        grid=(n_b // n_bt, n_len // t_n),
        in_specs=[
            pl.BlockSpec((n_bt, t_n, d), lambda i, t: (i, t, 0)),
            pl.BlockSpec((3, n_bt, 1, d), lambda i, t: (0, i, 0, 0)),
            _resident(norm_g.shape), _resident(w_u.shape), _resident(w_rest.shape),
            _resident(bc.shape),
            _resident(cc.shape), _resident(lam.shape), _resident(d_skip.shape),
            _resident(w_glu.shape), _resident(b_glu.shape), _resident(conv_w.shape),
            _resident(w_ps.shape), _resident(w_pc.shape), _resident(w_out.shape),
        ],
        out_specs=pl.BlockSpec((n_bt, t_n, d), lambda i, t: (i, t, 0)),
        out_shape=jax.ShapeDtypeStruct(x.shape, F32),
        scratch_shapes=[
            pltpu.VMEM((m, d), BF16),
            pltpu.VMEM((m, d_ssm), F32),
            pltpu.VMEM((d_ssm // LANES, m, LANES), F32),
            pltpu.VMEM((m, n_state_cols), F32),
            pltpu.VMEM((d_ssm // LANES, m, LANES), F32),
            pltpu.VMEM((m, d_ssm), F32),
            pltpu.VMEM((2, n_bt, n_state_cols // 2), F32),
            pltpu.VMEM((n_bt, t_n + SUBLANES, d_conv), F32),
            pltpu.VMEM((n_slice, m, MXU_COLS), F32),
        ],
        compiler_params=pltpu.CompilerParams(
            dimension_semantics=("arbitrary", "arbitrary"), vmem_limit_bytes=vmem),
        name="mixer",
    )(x, mod, norm_g, w_u, w_rest, bc, cc, lam, d_skip, w_glu, b_glu, conv_w, w_ps, w_pc, w_out)


def _mlp_kernel(x_ref, mod_ref, g_ref, w1_ref, w2_ref, fg_ref, o_ref, *, ff_chunk, final_norm):
    x = x_ref[0]
    h = _rms_normalize(x) * g_ref[...]
    h = (h * (1.0 + mod_ref[1, 0]) + mod_ref[0, 0]).astype(BF16)
    d_ff = w1_ref.shape[1]
    f = None
    for c0 in range(0, d_ff, ff_chunk):
        a = jnp.maximum(_dot(h, w1_ref[:, c0:c0 + ff_chunk]), 0.0)
        part = _dot((a * a).astype(BF16), w2_ref[c0:c0 + ff_chunk, :])
        f = part if f is None else f + part
    x2 = x + mod_ref[2, 0] * f
    o_ref[0] = _rms_normalize(x2) * fg_ref[...] if final_norm else x2


def _mlp(x, mod, norm_g, w1, w2, final_g, final_norm):
    n_b, n_len, d = x.shape
    _, _, rows, ff_chunk, _, _, _, vmem = _tiles(d)
    return pl.pallas_call(
        functools.partial(_mlp_kernel, ff_chunk=ff_chunk, final_norm=final_norm),
        grid=(n_b, n_len // rows),
        in_specs=[
            pl.BlockSpec((1, rows, d), lambda b, i: (b, i, 0)),
            pl.BlockSpec((3, 1, 1, d), lambda b, i: (1, b, 0, 0)),
            _resident(norm_g.shape), _resident(w1.shape), _resident(w2.shape),
            _resident(final_g.shape),
        ],
        out_specs=pl.BlockSpec((1, rows, d), lambda b, i: (b, i, 0)),
        out_shape=jax.ShapeDtypeStruct(x.shape, F32),
        compiler_params=pltpu.CompilerParams(
            dimension_semantics=("arbitrary", "arbitrary"), vmem_limit_bytes=vmem),
        name="mlp",
    )(x, mod, norm_g, w1, w2, final_g)


def _s5_operators(lam_re, lam_im, log_dt, b_re, b_im, c_re, c_im):
    n_g, n_p = lam_re.shape
    n_h = b_re.shape[-1]
    dt = jnp.exp(log_dt)[:, None]
    mag = jnp.exp(lam_re * dt)
    lb_re = mag * jnp.cos(lam_im * dt)
    lb_im = mag * jnp.sin(lam_im * dt)
    den = lam_re * lam_re + lam_im * lam_im
    k_re = ((lb_re - 1.0) * lam_re + lb_im * lam_im) / den
    k_im = (lb_im * lam_re - (lb_re - 1.0) * lam_im) / den
    bb_re = k_re[..., None] * b_re - k_im[..., None] * b_im
    bb_im = k_re[..., None] * b_im + k_im[..., None] * b_re

    g_half = n_g // 2
    n_pairs = g_half // 2
    eye = jnp.eye(g_half, dtype=F32)

    def b_half(half):
        sl = slice(half * g_half, (half + 1) * g_half)
        blk = jnp.stack([jnp.swapaxes(bb_re[sl], 1, 2), jnp.swapaxes(bb_im[sl], 1, 2)], axis=2)
        full = eye[:, None, :, None, None] * blk[:, :, None, :, :]
        full = full.reshape(g_half, n_h, n_pairs, 2, 2, n_p)
        full = jnp.swapaxes(full, 3, 4)
        return full.reshape(g_half * n_h, g_half * 2 * n_p)

    def c_half(half):
        sl = slice(half * g_half, (half + 1) * g_half)
        blk = jnp.stack([c_re[sl], -c_im[sl]], axis=2)
        full = eye[:, None, :, None, None] * blk[:, :, None, :, :]
        full = full.reshape(g_half, n_h, n_pairs, 2, 2, n_p)
        full = jnp.swapaxes(full, 3, 4)
        full = full.reshape(g_half * n_h, g_half * 2 * n_p)
        return full.T

    bc = jnp.stack([b_half(0), b_half(1)]).astype(BF16)
    cc = jnp.stack([c_half(0), c_half(1)]).astype(BF16)
    lam = jnp.stack([lb_re.reshape(-1), lb_im.reshape(-1)])
    lam = jnp.broadcast_to(lam[:, None, :], (2, SUBLANES, n_g * n_p))
    return lam, bc, cc


def kernel(x, c, norm1_g, norm2_g, w_ada, b_ada, w_in, lam_re, lam_im, log_dt, b_re, b_im,
           c_re, c_im, d_skip, w_glu, b_glu, conv_w, w_proj_ssm, w_proj_conv, w_out,
           w_ff1, w_ff2, final_g):
    depth = w_in.shape[0]
    n_b, _, d = x.shape
    for l in range(depth):
        mod = _adaln(c, w_ada[l], b_ada[l]).reshape(N_MOD, n_b, 1, d)
        lam, bc, cc = _s5_operators(lam_re[l], lam_im[l], log_dt[l], b_re[l], b_im[l],
                                    c_re[l], c_im[l])
        x = _mixer(x, mod, norm1_g[l][None], w_in[l].astype(BF16), bc, cc, lam,
                   d_skip[l][None], w_glu[l].astype(BF16), b_glu[l][None], conv_w[l],
                   w_proj_ssm[l].astype(BF16), w_proj_conv[l].astype(BF16),
                   w_out[l].astype(BF16))
        x = _mlp(x, mod, norm2_g[l][None], w_ff1[l].astype(BF16), w_ff2[l].astype(BF16),
                 final_g[None], final_norm=(l == depth - 1))
    return x
```

```python
import functools

import jax
import jax.numpy as jnp
from jax import lax
from jax.experimental import pallas as pl
from jax.experimental.pallas import tpu as pltpu

SUBLANES = 8
LANES = 128
MXU_COLS = 256
VMEM_BYTES_V7X = 64 * 1024 * 1024

RMS_EPS = 1e-6
N_MOD = 6
CONV_WIDTH = 3

F32 = jnp.float32
BF16 = jnp.bfloat16


def _tiles(d_model):
    batch_tile = SUBLANES
    time_tile = 64
    mlp_rows = 512
    mlp_slabs = 2
    ff_chunk = 1024
    n_stream = 1
    mixer_vmem = VMEM_BYTES_V7X - 8 * 1024 * 1024
    mlp_vmem = VMEM_BYTES_V7X - 8 * 1024 * 1024
    return (batch_tile, time_tile, mlp_rows, mlp_slabs, ff_chunk, n_stream,
            mixer_vmem, mlp_vmem)


def _rms_normalize(x):
    return x * lax.rsqrt(jnp.mean(x * x, axis=-1, keepdims=True) + RMS_EPS)


def _dot(a, b):
    return jnp.dot(a, b, preferred_element_type=F32)


def _adaln_kernel(c_ref, w_ref, b_ref, o_ref):
    c = c_ref[...]
    c_act = c * jax.nn.sigmoid(c)
    o_ref[0] = _dot(c_act.astype(BF16), w_ref[...].astype(BF16)) + b_ref[...]


def _adaln(c, w_ada, b_ada):
    n_b, d = c.shape
    return pl.pallas_call(
        _adaln_kernel,
        grid=(N_MOD,),
        in_specs=[
            pl.BlockSpec((n_b, d), lambda j: (0, 0)),
            pl.BlockSpec((d, d), lambda j: (0, j)),
            pl.BlockSpec((1, d), lambda j: (0, j)),
        ],
        out_specs=pl.BlockSpec((1, n_b, d), lambda j: (j, 0, 0)),
        out_shape=jax.ShapeDtypeStruct((N_MOD, n_b, d), F32),
        name="adaln",
    )(c, w_ada, b_ada.reshape(1, N_MOD * d))


def _mixer_kernel(x_ref, mod_ref, g_ref, w_u_ref, w_rest_ref, bc_ref, cc_ref, lam_ref, dskip_ref,
                  wglu_ref, bglu_ref, convw_ref, wps_ref, wpc_ref, wout_ref,
                  o_ref,
                  h_ref, u_ref, utb_ref, bu_ref, ytb_ref, ybt_ref, state_ref, vpad_ref, p_ref,
                  *, time_tile, n_stream, d_ssm, d_conv, d_model):
    t_n = time_tile
    n_bt = x_ref.shape[0]
    t_s = t_n // n_stream
    m_s = n_bt * t_s
    n_slab = d_ssm // LANES
    slabs_per_half = n_slab // 2
    half_cols = bu_ref.shape[1] // 2
    n_pair = bu_ref.shape[1] // (2 * LANES)
    n_slice = w_rest_ref.shape[0]

    @pl.when(pl.program_id(1) == 0)
    def _():
        state_ref[...] = jnp.zeros_like(state_ref)
        vpad_ref[:, 0:SUBLANES, :] = jnp.zeros((n_bt, SUBLANES, d_conv), F32)

    pair_cols = [p * LANES for p in range(n_pair)]
    lam_re = [lam_ref[0, :, c:c + LANES] for c in pair_cols]
    lam_im = [lam_ref[1, :, c:c + LANES] for c in pair_cols]
    x_re = [state_ref[0, :, c:c + LANES] for c in pair_cols]
    x_im = [state_ref[1, :, c:c + LANES] for c in pair_cols]

    for s in range(n_stream):
        r0 = s * m_s
        rows_s = slice(r0, r0 + m_s)
        t_lo = s * t_s

        for b in range(n_bt):
            xb = x_ref[b, t_lo:t_lo + t_s, :]
            hb = _rms_normalize(xb) * g_ref[...]
            hb = hb * (1.0 + mod_ref[1, b]) + mod_ref[0, b]
            h_ref[r0 + b * t_s:r0 + (b + 1) * t_s, :] = hb.astype(BF16)
        h = h_ref[rows_s, :]

        u_ref[rows_s, :] = _dot(h, w_u_ref[...])
        for b in range(n_bt):
            for j in range(n_slab):
                utb_ref[j, pl.ds(r0 + b, t_s, stride=n_bt), :] = (
                    u_ref[r0 + b * t_s:r0 + (b + 1) * t_s, j * LANES:(j + 1) * LANES])
        for half in range(2):
            lhs = jnp.concatenate(
                [utb_ref[half * slabs_per_half + j, rows_s, :] for j in range(slabs_per_half)],
                axis=1)
            bu_ref[rows_s, half * half_cols:(half + 1) * half_cols] = _dot(lhs.astype(BF16),
                                                                           bc_ref[half])

        for i in range(n_slice):
            p_ref[i, rows_s, :] = _dot(h, w_rest_ref[i])

        for t in range(t_s):
            rows = slice(r0 + t * n_bt, r0 + (t + 1) * n_bt)
            for p, c in enumerate(pair_cols):
                c_re, c_im = 2 * c, 2 * c + LANES
                n_re = lam_re[p] * x_re[p] - lam_im[p] * x_im[p] + bu_ref[rows, c_re:c_re + LANES]
                n_im = lam_re[p] * x_im[p] + lam_im[p] * x_re[p] + bu_ref[rows, c_im:c_im + LANES]
                bu_ref[rows, c_re:c_re + LANES] = n_re
                bu_ref[rows, c_im:c_im + LANES] = n_im
                x_re[p], x_im[p] = n_re, n_im

        for half in range(2):
            xs = bu_ref[rows_s, half * half_cols:(half + 1) * half_cols].astype(BF16)
            yh = _dot(xs, cc_ref[half])
            for j in range(slabs_per_half):
                ytb_ref[half * slabs_per_half + j, rows_s, :] = yh[:, j * LANES:(j + 1) * LANES]
        for b in range(n_bt):
            for j in range(n_slab):
                ybt_ref[r0 + b * t_s:r0 + (b + 1) * t_s, j * LANES:(j + 1) * LANES] = (
                    ytb_ref[j, pl.ds(r0 + b, t_s, stride=n_bt), :])

        ys = ybt_ref[rows_s, :] + dskip_ref[...] * u_ref[rows_s, :]
        ys = jax.nn.gelu(ys)
        ys = ys * jax.nn.sigmoid(_dot(ys.astype(BF16), wglu_ref[...]) + bglu_ref[...])

        def p_cols(first_col, n_cols, rows_s=rows_s):
            w = p_ref.shape[2]
            return jnp.concatenate(
                [p_ref[k, rows_s, :] for k in range(first_col // w, (first_col + n_cols) // w)],
                axis=1)

        cb = p_cols(0, d_conv)
        v = p_cols(d_conv, d_conv) * p_cols(2 * d_conv, d_conv)
        v3 = v.reshape(n_bt, t_s, d_conv)
        hist = SUBLANES + t_lo
        vpad_ref[:, hist:hist + t_s, :] = v3
        conv = convw_ref[CONV_WIDTH - 1:CONV_WIDTH, :] * v3
        for k in range(1, CONV_WIDTH):
            shifted = vpad_ref[:, hist - k:hist - k + t_s, :]
            conv = conv + convw_ref[CONV_WIDTH - 1 - k:CONV_WIDTH - k, :] * shifted
        yc = cb * conv.reshape(m_s, d_conv)

        c0 = 3 * d_conv
        merged = jax.nn.sigmoid(p_cols(c0, d_model)) * _dot(ys.astype(BF16), wps_ref[...])
        merged = merged + (jax.nn.sigmoid(p_cols(c0 + d_model, d_model))
                           * _dot(yc.astype(BF16), wpc_ref[...]))
        mixed = _dot(merged.astype(BF16), wout_ref[...])
        for b in range(n_bt):
            o_ref[b, t_lo:t_lo + t_s, :] = (x_ref[b, t_lo:t_lo + t_s, :]
                                            + mod_ref[2, b] * mixed[b * t_s:(b + 1) * t_s, :])

    for p, c in enumerate(pair_cols):
        state_ref[0, :, c:c + LANES] = x_re[p]
        state_ref[1, :, c:c + LANES] = x_im[p]
    vpad_ref[:, 0:SUBLANES, :] = vpad_ref[:, t_n:t_n + SUBLANES, :]


def _resident(shape):
    zeros = (0,) * len(shape)
    return pl.BlockSpec(shape, lambda *_: zeros, pipeline_mode=pl.Buffered(1))


def _mixer(x, mod, norm_g, w_in, bc, cc, lam, d_skip, w_glu, b_glu, conv_w, w_ps, w_pc, w_out):
    n_b, n_len, d = x.shape
    d_ssm = w_glu.shape[0]
    d_conv = conv_w.shape[1]
    n_state_cols = lam.shape[2] * 2
    n_bt, t_n, _, _, _, n_stream, vmem, _ = _tiles(d)
    m = n_bt * t_n
    w_u = w_in[:, :d_ssm]
    n_slice = (w_in.shape[1] - d_ssm) // MXU_COLS
    w_rest = jnp.swapaxes(w_in[:, d_ssm:].reshape(d, n_slice, MXU_COLS), 0, 1)
    kern = functools.partial(_mixer_kernel, time_tile=t_n, n_stream=n_stream,
                             d_ssm=d_ssm, d_conv=d_conv, d_model=d)
    return pl.pallas_call(
        kern,
        grid=(n_b // n_bt, n_len // t_n),
        in_specs=[
            pl.BlockSpec((n_bt, t_n, d), lambda i, t: (i, t, 0)),
            pl.BlockSpec((3, n_bt, 1, d), lambda i, t: (0, i, 0, 0)),
            _resident(norm_g.shape), _resident(w_u.shape), _resident(w_rest.shape),
            _resident(bc.shape),
            _resident(cc.shape), _resident(lam.shape), _resident(d_skip.shape),
            _resident(w_glu.shape), _resident(b_glu.shape), _resident(conv_w.shape),
            _resident(w_ps.shape), _resident(w_pc.shape), _resident(w_out.shape),
        ],
        out_specs=pl.BlockSpec((n_bt, t_n, d), lambda i, t: (i, t, 0)),
        out_shape=jax.ShapeDtypeStruct(x.shape, F32),
        scratch_shapes=[
            pltpu.VMEM((m, d), BF16),
            pltpu.VMEM((m, d_ssm), F32),
            pltpu.VMEM((d_ssm // LANES, m, LANES), F32),
            pltpu.VMEM((m, n_state_cols), F32),
            pltpu.VMEM((d_ssm // LANES, m, LANES), F32),
            pltpu.VMEM((m, d_ssm), F32),
            pltpu.VMEM((2, n_bt, n_state_cols // 2), F32),
            pltpu.VMEM((n_bt, t_n + SUBLANES, d_conv), F32),
            pltpu.VMEM((n_slice, m, MXU_COLS), F32),
        ],
        compiler_params=pltpu.CompilerParams(
            dimension_semantics=("arbitrary", "arbitrary"), vmem_limit_bytes=vmem),
        name="mixer",
    )(x, mod, norm_g, w_u, w_rest, bc, cc, lam, d_skip, w_glu, b_glu, conv_w, w_ps, w_pc, w_out)


def _mlp_kernel(x_ref, mod_ref, g_ref, w1_ref, w2_ref, fg_ref, o_ref,
                *, slab_rows, ff_chunk, final_norm):
    d_ff = w1_ref.shape[1]
    for r0 in range(0, x_ref.shape[1], slab_rows):
        x = x_ref[0, r0:r0 + slab_rows, :]
        h = _rms_normalize(x) * g_ref[...]
        h = (h * (1.0 + mod_ref[1, 0]) + mod_ref[0, 0]).astype(BF16)
        f = None
        for c0 in range(0, d_ff, ff_chunk):
            a = jnp.maximum(_dot(h, w1_ref[:, c0:c0 + ff_chunk]), 0.0)
            part = _dot((a * a).astype(BF16), w2_ref[c0:c0 + ff_chunk, :])
            f = part if f is None else f + part
        x2 = x + mod_ref[2, 0] * f
        o_ref[0, r0:r0 + slab_rows, :] = _rms_normalize(x2) * fg_ref[...] if final_norm else x2


def _mlp(x, mod, norm_g, w1, w2, final_g, final_norm):
    n_b, n_len, d = x.shape
    _, _, slab_rows, n_slabs, ff_chunk, _, _, vmem = _tiles(d)
    rows = slab_rows * n_slabs
    return pl.pallas_call(
        functools.partial(_mlp_kernel, slab_rows=slab_rows, ff_chunk=ff_chunk,
                          final_norm=final_norm),
        grid=(n_b, n_len // rows),
        in_specs=[
            pl.BlockSpec((1, rows, d), lambda b, i: (b, i, 0)),
            pl.BlockSpec((3, 1, 1, d), lambda b, i: (1, b, 0, 0)),
            _resident(norm_g.shape), _resident(w1.shape), _resident(w2.shape),
            _resident(final_g.shape),
        ],
        out_specs=pl.BlockSpec((1, rows, d), lambda b, i: (b, i, 0)),
        out_shape=jax.ShapeDtypeStruct(x.shape, F32),
        compiler_params=pltpu.CompilerParams(
            dimension_semantics=("arbitrary", "arbitrary"), vmem_limit_bytes=vmem),
        name="mlp",
    )(x, mod, norm_g, w1, w2, final_g)


def _s5_operators(lam_re, lam_im, log_dt, b_re, b_im, c_re, c_im):
    n_g, n_p = lam_re.shape
    n_h = b_re.shape[-1]
    dt = jnp.exp(log_dt)[:, None]
    mag = jnp.exp(lam_re * dt)
    lb_re = mag * jnp.cos(lam_im * dt)
    lb_im = mag * jnp.sin(lam_im * dt)
    den = lam_re * lam_re + lam_im * lam_im
    k_re = ((lb_re - 1.0) * lam_re + lb_im * lam_im) / den
    k_im = (lb_im * lam_re - (lb_re - 1.0) * lam_im) / den
    bb_re = k_re[..., None] * b_re - k_im[..., None] * b_im
    bb_im = k_re[..., None] * b_im + k_im[..., None] * b_re

    g_half = n_g // 2
    n_pairs = g_half // 2
    hot = jnp.eye(g_half, dtype=F32).reshape(g_half, n_pairs, 2)
    b_blk = jnp.stack([bb_re, bb_im]).reshape(2, 2, g_half, n_p, n_h).transpose(1, 2, 4, 0, 3)
    bc = (hot[None, :, None, :, None, :, None] * b_blk[:, :, :, None, :, None, :])
    bc = bc.reshape(2, g_half * n_h, g_half * 2 * n_p).astype(BF16)
    c_blk = jnp.stack([c_re, -c_im]).reshape(2, 2, g_half, n_h, n_p).transpose(1, 0, 4, 2, 3)
    cc = (hot.transpose(1, 2, 0)[None, :, None, :, None, :, None]
          * c_blk[:, None, :, None, :, :, :])
    cc = cc.reshape(2, g_half * 2 * n_p, g_half * n_h).astype(BF16)
    lam = jnp.stack([lb_re.reshape(-1), lb_im.reshape(-1)])
    lam = jnp.broadcast_to(lam[:, None, :], (2, SUBLANES, n_g * n_p))
    return lam, bc, cc


def kernel(x, c, norm1_g, norm2_g, w_ada, b_ada, w_in, lam_re, lam_im, log_dt, b_re, b_im,
           c_re, c_im, d_skip, w_glu, b_glu, conv_w, w_proj_ssm, w_proj_conv, w_out,
           w_ff1, w_ff2, final_g):
    depth = w_in.shape[0]
    n_b, _, d = x.shape
    for l in range(depth):
        mod = _adaln(c, w_ada[l], b_ada[l]).reshape(N_MOD, n_b, 1, d)
        lam, bc, cc = _s5_operators(lam_re[l], lam_im[l], log_dt[l], b_re[l], b_im[l],
                                    c_re[l], c_im[l])
        x = _mixer(x, mod, norm1_g[l][None], w_in[l].astype(BF16), bc, cc, lam,
                   d_skip[l][None], w_glu[l].astype(BF16), b_glu[l][None], conv_w[l],
                   w_proj_ssm[l].astype(BF16), w_proj_conv[l].astype(BF16),
                   w_out[l].astype(BF16))
        x = _mlp(x, mod, norm2_g[l][None], w_ff1[l].astype(BF16), w_ff2[l].astype(BF16),
                 final_g[None], final_norm=(l == depth - 1))
    return x
```

```python
import functools

import jax
import jax.numpy as jnp
from jax import lax
from jax.experimental import pallas as pl
from jax.experimental.pallas import tpu as pltpu

SUBLANES = 8
BF16_ROWS = 16
LANES = 128
MXU_COLS = 256
VMEM_BYTES_V7X = 64 * 1024 * 1024

RMS_EPS = 1e-6
N_MOD = 6
CONV_WIDTH = 3

F32 = jnp.float32
BF16 = jnp.bfloat16


def _tiles(d_model):
    batch_tile = SUBLANES
    time_tile = 64
    mlp_rows = 512
    mlp_slabs = 2
    ff_chunk = 1024
    mixer_vmem = VMEM_BYTES_V7X - 8 * 1024 * 1024
    mlp_vmem = VMEM_BYTES_V7X - 8 * 1024 * 1024
    return batch_tile, time_tile, mlp_rows, mlp_slabs, ff_chunk, mixer_vmem, mlp_vmem


def _rms_normalize(x):
    return x * lax.rsqrt(jnp.mean(x * x, axis=-1, keepdims=True) + RMS_EPS)


def _dot(a, b):
    return jnp.dot(a, b, preferred_element_type=F32)


def _adaln_kernel(c_ref, w_ref, b_ref, o_ref):
    c = c_ref[...]
    c_act = c * jax.nn.sigmoid(c)
    o_ref[0] = _dot(c_act.astype(BF16), w_ref[...].astype(BF16)) + b_ref[...]


def _adaln(c, w_ada, b_ada):
    n_b, d = c.shape
    return pl.pallas_call(
        _adaln_kernel,
        grid=(N_MOD,),
        in_specs=[
            pl.BlockSpec((n_b, d), lambda j: (0, 0)),
            pl.BlockSpec((d, d), lambda j: (0, j)),
            pl.BlockSpec((1, d), lambda j: (0, j)),
        ],
        out_specs=pl.BlockSpec((1, n_b, d), lambda j: (j, 0, 0)),
        out_shape=jax.ShapeDtypeStruct((N_MOD, n_b, d), F32),
        name="adaln",
    )(c, w_ada, b_ada.reshape(1, N_MOD * d))


def _mixer_kernel(x_ref, mod_ref, g_ref, w_in_ref, bc_ref, cc_ref, lam_ref, dskip_ref,
                  wglu_ref, bglu_ref, convw_ref, wps_ref, wpc_ref, wout_ref,
                  o_ref,
                  h_ref, u_ref, utb_ref, bu_ref, xs_ref, ytb_ref, ybt_ref, state_ref, vpad_ref, p_ref,
                  *, time_tile, d_ssm, d_conv, d_model):
    t_n = time_tile
    n_bt = x_ref.shape[0]
    m = n_bt * t_n
    n_slab = d_ssm // LANES
    slabs_per_half = n_slab // 2
    half_cols = bu_ref.shape[1] // 2
    n_pair = bu_ref.shape[1] // (2 * LANES)
    w_p = p_ref.shape[2]
    row_halves = [slice(0, m // 2), slice(m // 2, m)]
    batch_halves = [range(0, n_bt // 2), range(n_bt // 2, n_bt)]

    @pl.when(pl.program_id(1) == 0)
    def _():
        state_ref[...] = jnp.zeros_like(state_ref)
        vpad_ref[:, 0:SUBLANES, :] = jnp.zeros((n_bt, SUBLANES, d_conv), F32)


    def project(first_col, n_cols):
        for k in range(first_col // w_p, (first_col + n_cols) // w_p):
            c0 = d_ssm + k * w_p
            p_ref[k] = _dot(h_ref[...], w_in_ref[:, c0:c0 + w_p])

    def p_cols(first_col, n_cols, rows=slice(None)):
        return jnp.concatenate(
            [p_ref[k, rows, :] for k in range(first_col // w_p, (first_col + n_cols) // w_p)],
            axis=1)

    for rows, batches in zip(row_halves, batch_halves):
        for b in batches:
            hb = _rms_normalize(x_ref[b]) * g_ref[...]
            hb = hb * (1.0 + mod_ref[1, b]) + mod_ref[0, b]
            h_ref[b * t_n:(b + 1) * t_n, :] = hb.astype(BF16)
        u_ref[rows, :] = _dot(h_ref[rows, :], w_in_ref[:, 0:d_ssm])

    project(0, 3 * d_conv)

    for b in range(n_bt):
        for j in range(n_slab):
            utb_ref[j, pl.ds(b, t_n, stride=n_bt), :] = (
                u_ref[b * t_n:(b + 1) * t_n, j * LANES:(j + 1) * LANES])
    for half in range(2):
        lhs = jnp.concatenate(
            [utb_ref[half * slabs_per_half + j] for j in range(slabs_per_half)], axis=1)
        bu_ref[:, half * half_cols:(half + 1) * half_cols] = _dot(lhs.astype(BF16), bc_ref[half])

    v3 = (p_cols(d_conv, d_conv) * p_cols(2 * d_conv, d_conv)).reshape(n_bt, t_n, d_conv)
    vpad_ref[:, SUBLANES:SUBLANES + t_n, :] = v3
    conv = convw_ref[CONV_WIDTH - 1:CONV_WIDTH, :] * v3
    for k in range(1, CONV_WIDTH):
        shifted = vpad_ref[:, SUBLANES - k:SUBLANES - k + t_n, :]
        conv = conv + convw_ref[CONV_WIDTH - 1 - k:CONV_WIDTH - k, :] * shifted
    vpad_ref[:, 0:SUBLANES, :] = vpad_ref[:, t_n:t_n + SUBLANES, :]
    yc = p_cols(0, d_conv) * conv.reshape(m, d_conv)

    c_gate = 3 * d_conv
    project(c_gate, 2 * d_model - 2 * w_p)
    conv_out = _dot(yc.astype(BF16), wpc_ref[...])

    pair_cols = [p * LANES for p in range(n_pair)]
    lam_re = [lam_ref[0, :, c:c + LANES] for c in pair_cols]
    lam_im = [lam_ref[1, :, c:c + LANES] for c in pair_cols]
    x_re = [state_ref[0, :, c:c + LANES] for c in pair_cols]
    x_im = [state_ref[1, :, c:c + LANES] for c in pair_cols]
    steps_per_tile = BF16_ROWS // n_bt
    for t0 in range(0, t_n, steps_per_tile):
        tile_rows = slice(t0 * n_bt, (t0 + steps_per_tile) * n_bt)
        for p, c in enumerate(pair_cols):
            c_re, c_im = 2 * c, 2 * c + LANES
            new_re, new_im = [], []
            for t in range(t0, t0 + steps_per_tile):
                rows = slice(t * n_bt, (t + 1) * n_bt)
                n_re = lam_re[p] * x_re[p] - lam_im[p] * x_im[p] + bu_ref[rows, c_re:c_re + LANES]
                n_im = lam_re[p] * x_im[p] + lam_im[p] * x_re[p] + bu_ref[rows, c_im:c_im + LANES]
                x_re[p], x_im[p] = n_re, n_im
                new_re.append(n_re)
                new_im.append(n_im)
            xs_ref[tile_rows, c_re:c_re + LANES] = jnp.concatenate(new_re, axis=0).astype(BF16)
            xs_ref[tile_rows, c_im:c_im + LANES] = jnp.concatenate(new_im, axis=0).astype(BF16)
    for p, c in enumerate(pair_cols):
        state_ref[0, :, c:c + LANES] = x_re[p]
        state_ref[1, :, c:c + LANES] = x_im[p]

    for half in range(2):
        yh = _dot(xs_ref[:, half * half_cols:(half + 1) * half_cols], cc_ref[half])
        for j in range(slabs_per_half):
            ytb_ref[half * slabs_per_half + j] = yh[:, j * LANES:(j + 1) * LANES]

    project(c_gate + 2 * d_model - 2 * w_p, w_p)

    for b in range(n_bt):
        for j in range(n_slab):
            ybt_ref[b * t_n:(b + 1) * t_n, j * LANES:(j + 1) * LANES] = (
                ytb_ref[j, pl.ds(b, t_n, stride=n_bt), :])
    ys = ybt_ref[...] + dskip_ref[...] * u_ref[...]
    ys = jax.nn.gelu(ys)
    glu = _dot(ys.astype(BF16), wglu_ref[...])

    project(c_gate + 2 * d_model - w_p, w_p)

    ys = (ys * jax.nn.sigmoid(glu + bglu_ref[...])).astype(BF16)

    merged = []
    for rows in row_halves:
        ssm_out = _dot(ys[rows, :], wps_ref[...])
        merged.append(jax.nn.sigmoid(p_cols(c_gate, d_model, rows)) * ssm_out
                      + jax.nn.sigmoid(p_cols(c_gate + d_model, d_model, rows)) * conv_out[rows, :])
    for rows, batches, mg in zip(row_halves, batch_halves, merged):
        mixed = _dot(mg.astype(BF16), wout_ref[...])
        r0 = rows.start
        for b in batches:
            o_ref[b] = x_ref[b] + mod_ref[2, b] * mixed[b * t_n - r0:(b + 1) * t_n - r0, :]


def _resident(shape):
    zeros = (0,) * len(shape)
    return pl.BlockSpec(shape, lambda *_: zeros, pipeline_mode=pl.Buffered(1))


def _mixer(x, mod, norm_g, w_in, bc, cc, lam, d_skip, w_glu, b_glu, conv_w, w_ps, w_pc, w_out):
    n_b, n_len, d = x.shape
    d_ssm = w_glu.shape[0]
    d_conv = conv_w.shape[1]
    n_state_cols = lam.shape[2] * 2
    n_bt, t_n, _, _, _, vmem, _ = _tiles(d)
    m = n_bt * t_n
    n_slice = (w_in.shape[1] - d_ssm) // MXU_COLS
    kern = functools.partial(_mixer_kernel, time_tile=t_n, d_ssm=d_ssm, d_conv=d_conv, d_model=d)
    return pl.pallas_call(
        kern,
        grid=(n_b // n_bt, n_len // t_n),
        in_specs=[
            pl.BlockSpec((n_bt, t_n, d), lambda i, t: (i, t, 0)),
            pl.BlockSpec((3, n_bt, 1, d), lambda i, t: (0, i, 0, 0)),
            _resident(norm_g.shape), _resident(w_in.shape), _resident(bc.shape),
            _resident(cc.shape), _resident(lam.shape), _resident(d_skip.shape),
            _resident(w_glu.shape), _resident(b_glu.shape), _resident(conv_w.shape),
            _resident(w_ps.shape), _resident(w_pc.shape), _resident(w_out.shape),
        ],
        out_specs=pl.BlockSpec((n_bt, t_n, d), lambda i, t: (i, t, 0)),
        out_shape=jax.ShapeDtypeStruct(x.shape, F32),
        scratch_shapes=[
            pltpu.VMEM((m, d), BF16),
            pltpu.VMEM((m, d_ssm), F32),
            pltpu.VMEM((d_ssm // LANES, m, LANES), F32),
            pltpu.VMEM((m, n_state_cols), F32),
            pltpu.VMEM((m, n_state_cols), BF16),
            pltpu.VMEM((d_ssm // LANES, m, LANES), F32),
            pltpu.VMEM((m, d_ssm), F32),
            pltpu.VMEM((2, n_bt, n_state_cols // 2), F32),
            pltpu.VMEM((n_bt, t_n + SUBLANES, d_conv), F32),
            pltpu.VMEM((n_slice, m, MXU_COLS), F32),
        ],
        compiler_params=pltpu.CompilerParams(
            dimension_semantics=("arbitrary", "arbitrary"), vmem_limit_bytes=vmem),
        name="mixer",
    )(x, mod, norm_g, w_in, bc, cc, lam, d_skip, w_glu, b_glu, conv_w, w_ps, w_pc, w_out)


def _mlp_kernel(x_ref, mod_ref, g_ref, w1_ref, w2_ref, fg_ref, o_ref,
                *, slab_rows, ff_chunk, final_norm):
    d_ff = w1_ref.shape[1]
    for r0 in range(0, x_ref.shape[1], slab_rows):
        x = x_ref[0, r0:r0 + slab_rows, :]
        h = _rms_normalize(x) * g_ref[...]
        h = (h * (1.0 + mod_ref[1, 0]) + mod_ref[0, 0]).astype(BF16)
        f = None
        for c0 in range(0, d_ff, ff_chunk):
            a = jnp.maximum(_dot(h, w1_ref[:, c0:c0 + ff_chunk]), 0.0)
            part = _dot((a * a).astype(BF16), w2_ref[c0:c0 + ff_chunk, :])
            f = part if f is None else f + part
        x2 = x + mod_ref[2, 0] * f
        o_ref[0, r0:r0 + slab_rows, :] = _rms_normalize(x2) * fg_ref[...] if final_norm else x2


def _mlp(x, mod, norm_g, w1, w2, final_g, final_norm):
    n_b, n_len, d = x.shape
    _, _, slab_rows, n_slabs, ff_chunk, _, vmem = _tiles(d)
    rows = slab_rows * n_slabs
    return pl.pallas_call(
        functools.partial(_mlp_kernel, slab_rows=slab_rows, ff_chunk=ff_chunk,
                          final_norm=final_norm),
        grid=(n_b, n_len // rows),
        in_specs=[
            pl.BlockSpec((1, rows, d), lambda b, i: (b, i, 0)),
            pl.BlockSpec((3, 1, 1, d), lambda b, i: (1, b, 0, 0)),
            _resident(norm_g.shape), _resident(w1.shape), _resident(w2.shape),
            _resident(final_g.shape),
        ],
        out_specs=pl.BlockSpec((1, rows, d), lambda b, i: (b, i, 0)),
        out_shape=jax.ShapeDtypeStruct(x.shape, F32),
        compiler_params=pltpu.CompilerParams(
            dimension_semantics=("arbitrary", "arbitrary"), vmem_limit_bytes=vmem),
        name="mlp",
    )(x, mod, norm_g, w1, w2, final_g)


def _s5_operators(lam_re, lam_im, log_dt, b_re, b_im, c_re, c_im):
    n_g, n_p = lam_re.shape
    n_h = b_re.shape[-1]
    dt = jnp.exp(log_dt)[:, None]
    mag = jnp.exp(lam_re * dt)
    lb_re = mag * jnp.cos(lam_im * dt)
    lb_im = mag * jnp.sin(lam_im * dt)
    den = lam_re * lam_re + lam_im * lam_im
    k_re = ((lb_re - 1.0) * lam_re + lb_im * lam_im) / den
    k_im = (lb_im * lam_re - (lb_re - 1.0) * lam_im) / den
    bb_re = k_re[..., None] * b_re - k_im[..., None] * b_im
    bb_im = k_re[..., None] * b_im + k_im[..., None] * b_re

    g_half = n_g // 2
    n_pairs = g_half // 2
    n_rows = g_half * n_h
    n_cols = g_half * 2 * n_p
    b_tab = jnp.stack([bb_re, bb_im]).reshape(2, 2, n_pairs, 2, n_p, n_h)
    b_tab = b_tab.transpose(1, 5, 2, 0, 3, 4).reshape(2, n_h, n_cols)
    c_tab = jnp.stack([c_re, -c_im]).reshape(2, 2, n_pairs, 2, n_h, n_p)
    c_tab = c_tab.transpose(1, 4, 2, 0, 3, 5).reshape(2, n_h, n_cols)
    row_group = lax.broadcasted_iota(jnp.int32, (n_rows, n_cols), 0) // n_h
    col = lax.broadcasted_iota(jnp.int32, (n_rows, n_cols), 1)
    col_group = (col // (4 * n_p)) * 2 + (col // n_p) % 2
    on_diagonal = row_group == col_group

    def block_diagonal(tab):
        full = jnp.broadcast_to(tab[:, None], (2, g_half, n_h, n_cols)).reshape(2, n_rows, n_cols)
        return jnp.where(on_diagonal[None], full, 0.0).astype(BF16)

    bc = block_diagonal(b_tab)
    cc = jnp.swapaxes(block_diagonal(c_tab), 1, 2)
    lam = jnp.stack([lb_re.reshape(-1), lb_im.reshape(-1)])
    lam = jnp.broadcast_to(lam[:, None, :], (2, SUBLANES, n_g * n_p))
    return lam, bc, cc


def kernel(x, c, norm1_g, norm2_g, w_ada, b_ada, w_in, lam_re, lam_im, log_dt, b_re, b_im,
           c_re, c_im, d_skip, w_glu, b_glu, conv_w, w_proj_ssm, w_proj_conv, w_out,
           w_ff1, w_ff2, final_g):
    depth = w_in.shape[0]
    n_b, _, d = x.shape
    for l in range(depth):
        mod = _adaln(c, w_ada[l], b_ada[l]).reshape(N_MOD, n_b, 1, d)
        lam, bc, cc = _s5_operators(lam_re[l], lam_im[l], log_dt[l], b_re[l], b_im[l],
                                    c_re[l], c_im[l])
        x = _mixer(x, mod, norm1_g[l][None], w_in[l].astype(BF16), bc, cc, lam,
                   d_skip[l][None], w_glu[l].astype(BF16), b_glu[l][None], conv_w[l],
                   w_proj_ssm[l].astype(BF16), w_proj_conv[l].astype(BF16),
                   w_out[l].astype(BF16))
        x = _mlp(x, mod, norm2_g[l][None], w_ff1[l].astype(BF16), w_ff2[l].astype(BF16),
                 final_g[None], final_norm=(l == depth - 1))
    return x
```

```python
import functools

import jax
import jax.numpy as jnp
from jax import lax
from jax.experimental import pallas as pl
from jax.experimental.pallas import tpu as pltpu

SUBLANES = 8
BF16_ROWS = 16
LANES = 128
MXU_COLS = 256
VMEM_BYTES_V7X = 64 * 1024 * 1024

RMS_EPS = 1e-6
N_MOD = 6
CONV_WIDTH = 3

F32 = jnp.float32
BF16 = jnp.bfloat16


def _tiles(d_model):
    batch_tile = SUBLANES
    time_tile = 64
    mlp_rows = 512
    mlp_slabs = 2
    ff_chunk = 1024
    mixer_vmem = VMEM_BYTES_V7X - 8 * 1024 * 1024
    mlp_vmem = VMEM_BYTES_V7X - 8 * 1024 * 1024
    return batch_tile, time_tile, mlp_rows, mlp_slabs, ff_chunk, mixer_vmem, mlp_vmem


def _rms_normalize(x):
    return x * lax.rsqrt(jnp.mean(x * x, axis=-1, keepdims=True) + RMS_EPS)


def _dot(a, b):
    return jnp.dot(a, b, preferred_element_type=F32)


def _adaln_kernel(c_ref, w_ref, b_ref, o_ref):
    c = c_ref[...]
    c_act = c * jax.nn.sigmoid(c)
    o_ref[0] = _dot(c_act.astype(BF16), w_ref[...].astype(BF16)) + b_ref[...]


def _adaln(c, w_ada, b_ada):
    n_b, d = c.shape
    return pl.pallas_call(
        _adaln_kernel,
        grid=(N_MOD,),
        in_specs=[
            pl.BlockSpec((n_b, d), lambda j: (0, 0)),
            pl.BlockSpec((d, d), lambda j: (0, j)),
            pl.BlockSpec((1, d), lambda j: (0, j)),
        ],
        out_specs=pl.BlockSpec((1, n_b, d), lambda j: (j, 0, 0)),
        out_shape=jax.ShapeDtypeStruct((N_MOD, n_b, d), F32),
        name="adaln",
    )(c, w_ada, b_ada.reshape(1, N_MOD * d))


def _mixer_kernel(x_ref, mod_ref, g_ref, w_in_ref, bc_ref, cc_ref, lam_ref, dskip_ref,
                  wglu_ref, bglu_ref, convw_ref, wps_ref, wpc_ref, wout_ref,
                  o_ref,
                  h_ref, u_ref, utb_ref, bu_ref, xs_ref, ytb_ref, ybt_ref, state_ref, vpad_ref, p_ref,
                  *, time_tile, d_ssm, d_conv, d_model):
    t_n = time_tile
    n_bt = x_ref.shape[0]
    m = n_bt * t_n
    n_slab = d_ssm // LANES
    slabs_per_half = n_slab // 2
    half_cols = bu_ref.shape[1] // 2
    n_pair = bu_ref.shape[1] // (2 * LANES)
    w_p = p_ref.shape[2]
    row_halves = [slice(0, m // 2), slice(m // 2, m)]
    batch_halves = [range(0, n_bt // 2), range(n_bt // 2, n_bt)]

    @pl.when(pl.program_id(1) == 0)
    def _():
        state_ref[...] = jnp.zeros_like(state_ref)
        vpad_ref[:, 0:SUBLANES, :] = jnp.zeros((n_bt, SUBLANES, d_conv), F32)


    def project(first_col, n_cols):
        for k in range(first_col // w_p, (first_col + n_cols) // w_p):
            c0 = d_ssm + k * w_p
            p_ref[k] = _dot(h_ref[...], w_in_ref[:, c0:c0 + w_p])

    def p_cols(first_col, n_cols, rows=slice(None)):
        return jnp.concatenate(
            [p_ref[k, rows, :] for k in range(first_col // w_p, (first_col + n_cols) // w_p)],
            axis=1)

    for rows, batches in zip(row_halves, batch_halves):
        for b in batches:
            hb = _rms_normalize(x_ref[b]) * g_ref[...]
            hb = hb * (1.0 + mod_ref[1, b]) + mod_ref[0, b]
            h_ref[b * t_n:(b + 1) * t_n, :] = hb.astype(BF16)
        u_ref[rows, :] = _dot(h_ref[rows, :], w_in_ref[:, 0:d_ssm])

    c_gate = 3 * d_conv
    n_early_gate = 2 * w_p
    project(0, d_conv)
    project(c_gate, n_early_gate)

    for b in range(n_bt):
        for j in range(n_slab):
            utb_ref[j, pl.ds(b, t_n, stride=n_bt), :] = (
                u_ref[b * t_n:(b + 1) * t_n, j * LANES:(j + 1) * LANES])
    for half in range(2):
        lhs = jnp.concatenate(
            [utb_ref[half * slabs_per_half + j] for j in range(slabs_per_half)], axis=1)
        bu_ref[:, half * half_cols:(half + 1) * half_cols] = _dot(lhs.astype(BF16), bc_ref[half])

    project(d_conv, 2 * d_conv)
    v3 = (p_cols(d_conv, d_conv) * p_cols(2 * d_conv, d_conv)).reshape(n_bt, t_n, d_conv)
    vpad_ref[:, SUBLANES:SUBLANES + t_n, :] = v3
    conv = convw_ref[CONV_WIDTH - 1:CONV_WIDTH, :] * v3
    for k in range(1, CONV_WIDTH):
        shifted = vpad_ref[:, SUBLANES - k:SUBLANES - k + t_n, :]
        conv = conv + convw_ref[CONV_WIDTH - 1 - k:CONV_WIDTH - k, :] * shifted
    vpad_ref[:, 0:SUBLANES, :] = vpad_ref[:, t_n:t_n + SUBLANES, :]

    project(c_gate + n_early_gate, 2 * d_model - n_early_gate - 2 * w_p)
    yc = p_cols(0, d_conv) * conv.reshape(m, d_conv)
    conv_out = _dot(yc.astype(BF16), wpc_ref[...])

    pair_cols = [p * LANES for p in range(n_pair)]
    lam_re = [lam_ref[0, :, c:c + LANES] for c in pair_cols]
    lam_im = [lam_ref[1, :, c:c + LANES] for c in pair_cols]
    x_re = [state_ref[0, :, c:c + LANES] for c in pair_cols]
    x_im = [state_ref[1, :, c:c + LANES] for c in pair_cols]
    steps_per_tile = BF16_ROWS // n_bt
    for t0 in range(0, t_n, steps_per_tile):
        tile_rows = slice(t0 * n_bt, (t0 + steps_per_tile) * n_bt)
        for p, c in enumerate(pair_cols):
            c_re, c_im = 2 * c, 2 * c + LANES
            new_re, new_im = [], []
            for t in range(t0, t0 + steps_per_tile):
                rows = slice(t * n_bt, (t + 1) * n_bt)
                n_re = lam_re[p] * x_re[p] - lam_im[p] * x_im[p] + bu_ref[rows, c_re:c_re + LANES]
                n_im = lam_re[p] * x_im[p] + lam_im[p] * x_re[p] + bu_ref[rows, c_im:c_im + LANES]
                x_re[p], x_im[p] = n_re, n_im
                new_re.append(n_re)
                new_im.append(n_im)
            xs_ref[tile_rows, c_re:c_re + LANES] = jnp.concatenate(new_re, axis=0).astype(BF16)
            xs_ref[tile_rows, c_im:c_im + LANES] = jnp.concatenate(new_im, axis=0).astype(BF16)
    for p, c in enumerate(pair_cols):
        state_ref[0, :, c:c + LANES] = x_re[p]
        state_ref[1, :, c:c + LANES] = x_im[p]

    for half in range(2):
        yh = _dot(xs_ref[:, half * half_cols:(half + 1) * half_cols], cc_ref[half])
        for j in range(slabs_per_half):
            ytb_ref[half * slabs_per_half + j] = yh[:, j * LANES:(j + 1) * LANES]

    project(c_gate + 2 * d_model - 2 * w_p, w_p)

    for b in range(n_bt):
        for j in range(n_slab):
            ybt_ref[b * t_n:(b + 1) * t_n, j * LANES:(j + 1) * LANES] = (
                ytb_ref[j, pl.ds(b, t_n, stride=n_bt), :])
    ys = ybt_ref[...] + dskip_ref[...] * u_ref[...]
    ys = jax.nn.gelu(ys)
    glu = _dot(ys.astype(BF16), wglu_ref[...])

    project(c_gate + 2 * d_model - w_p, w_p)

    ys = (ys * jax.nn.sigmoid(glu + bglu_ref[...])).astype(BF16)

    merged = []
    for rows in row_halves:
        ssm_out = _dot(ys[rows, :], wps_ref[...])
        merged.append(jax.nn.sigmoid(p_cols(c_gate, d_model, rows)) * ssm_out
                      + jax.nn.sigmoid(p_cols(c_gate + d_model, d_model, rows)) * conv_out[rows, :])
    for rows, batches, mg in zip(row_halves, batch_halves, merged):
        mixed = _dot(mg.astype(BF16), wout_ref[...])
        r0 = rows.start
        for b in batches:
            o_ref[b] = x_ref[b] + mod_ref[2, b] * mixed[b * t_n - r0:(b + 1) * t_n - r0, :]


def _resident(shape):
    zeros = (0,) * len(shape)
    return pl.BlockSpec(shape, lambda *_: zeros, pipeline_mode=pl.Buffered(1))


def _mixer(x, mod, norm_g, w_in, bc, cc, lam, d_skip, w_glu, b_glu, conv_w, w_ps, w_pc, w_out):
    n_b, n_len, d = x.shape
    d_ssm = w_glu.shape[0]
    d_conv = conv_w.shape[1]
    n_state_cols = lam.shape[2] * 2
    n_bt, t_n, _, _, _, vmem, _ = _tiles(d)
    m = n_bt * t_n
    n_slice = (w_in.shape[1] - d_ssm) // MXU_COLS
    kern = functools.partial(_mixer_kernel, time_tile=t_n, d_ssm=d_ssm, d_conv=d_conv, d_model=d)
    return pl.pallas_call(
        kern,
        grid=(n_b // n_bt, n_len // t_n),
        in_specs=[
            pl.BlockSpec((n_bt, t_n, d), lambda i, t: (i, t, 0)),
            pl.BlockSpec((3, n_bt, 1, d), lambda i, t: (0, i, 0, 0)),
            _resident(norm_g.shape), _resident(w_in.shape), _resident(bc.shape),
            _resident(cc.shape), _resident(lam.shape), _resident(d_skip.shape),
            _resident(w_glu.shape), _resident(b_glu.shape), _resident(conv_w.shape),
            _resident(w_ps.shape), _resident(w_pc.shape), _resident(w_out.shape),
        ],
        out_specs=pl.BlockSpec((n_bt, t_n, d), lambda i, t: (i, t, 0)),
        out_shape=jax.ShapeDtypeStruct(x.shape, F32),
        scratch_shapes=[
            pltpu.VMEM((m, d), BF16),
            pltpu.VMEM((m, d_ssm), F32),
            pltpu.VMEM((d_ssm // LANES, m, LANES), F32),
            pltpu.VMEM((m, n_state_cols), F32),
            pltpu.VMEM((m, n_state_cols), BF16),
            pltpu.VMEM((d_ssm // LANES, m, LANES), F32),
            pltpu.VMEM((m, d_ssm), F32),
            pltpu.VMEM((2, n_bt, n_state_cols // 2), F32),
            pltpu.VMEM((n_bt, t_n + SUBLANES, d_conv), F32),
            pltpu.VMEM((n_slice, m, MXU_COLS), F32),
        ],
        compiler_params=pltpu.CompilerParams(
            dimension_semantics=("arbitrary", "arbitrary"), vmem_limit_bytes=vmem),
        name="mixer",
    )(x, mod, norm_g, w_in, bc, cc, lam, d_skip, w_glu, b_glu, conv_w, w_ps, w_pc, w_out)


def _mlp_kernel(x_ref, mod_ref, g_ref, w1_ref, w2_ref, fg_ref, o_ref,
                *, slab_rows, ff_chunk, final_norm):
    d_ff = w1_ref.shape[1]
    chunks = list(range(0, d_ff, ff_chunk))
    slabs = list(range(0, x_ref.shape[1], slab_rows))
    x, h, f, act = {}, {}, {}, {}

    def up(r0, c0):
        a = jnp.maximum(_dot(h[r0], w1_ref[:, c0:c0 + ff_chunk]), 0.0)
        act[r0, c0] = (a * a).astype(BF16)

    def down(r0, c0):
        part = _dot(act.pop((r0, c0)), w2_ref[c0:c0 + ff_chunk, :])
        f[r0] = part if r0 not in f else f[r0] + part

    def finish(r0):
        x2 = x[r0] + mod_ref[2, 0] * f[r0]
        o_ref[0, r0:r0 + slab_rows, :] = _rms_normalize(x2) * fg_ref[...] if final_norm else x2

    steps = [(r0, c0) for r0 in slabs for c0 in chunks]
    for r0 in slabs:
        x[r0] = x_ref[0, r0:r0 + slab_rows, :]
        hn = _rms_normalize(x[r0]) * g_ref[...]
        h[r0] = (hn * (1.0 + mod_ref[1, 0]) + mod_ref[0, 0]).astype(BF16)
    up(*steps[0])
    for prev, cur in zip(steps, steps[1:]):
        up(*cur)
        down(*prev)
        if prev[1] == chunks[-1]:
            finish(prev[0])
    down(*steps[-1])
    finish(steps[-1][0])


def _mlp(x, mod, norm_g, w1, w2, final_g, final_norm):
    n_b, n_len, d = x.shape
    _, _, slab_rows, n_slabs, ff_chunk, _, vmem = _tiles(d)
    rows = slab_rows * n_slabs
    return pl.pallas_call(
        functools.partial(_mlp_kernel, slab_rows=slab_rows, ff_chunk=ff_chunk,
                          final_norm=final_norm),
        grid=(n_b, n_len // rows),
        in_specs=[
            pl.BlockSpec((1, rows, d), lambda b, i: (b, i, 0)),
            pl.BlockSpec((3, 1, 1, d), lambda b, i: (1, b, 0, 0)),
            _resident(norm_g.shape), _resident(w1.shape), _resident(w2.shape),
            _resident(final_g.shape),
        ],
        out_specs=pl.BlockSpec((1, rows, d), lambda b, i: (b, i, 0)),
        out_shape=jax.ShapeDtypeStruct(x.shape, F32),
        compiler_params=pltpu.CompilerParams(
            dimension_semantics=("arbitrary", "arbitrary"), vmem_limit_bytes=vmem),
        name="mlp",
    )(x, mod, norm_g, w1, w2, final_g)


def _s5_operators(lam_re, lam_im, log_dt, b_re, b_im, c_re, c_im):
    n_g, n_p = lam_re.shape
    n_h = b_re.shape[-1]
    dt = jnp.exp(log_dt)[:, None]
    mag = jnp.exp(lam_re * dt)
    lb_re = mag * jnp.cos(lam_im * dt)
    lb_im = mag * jnp.sin(lam_im * dt)
    den = lam_re * lam_re + lam_im * lam_im
    k_re = ((lb_re - 1.0) * lam_re + lb_im * lam_im) / den
    k_im = (lb_im * lam_re - (lb_re - 1.0) * lam_im) / den
    bb_re = k_re[..., None] * b_re - k_im[..., None] * b_im
    bb_im = k_re[..., None] * b_im + k_im[..., None] * b_re

    g_half = n_g // 2
    n_pairs = g_half // 2
    n_rows = g_half * n_h
    n_cols = g_half * 2 * n_p
    b_tab = jnp.stack([bb_re, bb_im]).reshape(2, 2, n_pairs, 2, n_p, n_h)
    b_tab = b_tab.transpose(1, 5, 2, 0, 3, 4).reshape(2, n_h, n_cols)
    c_tab = jnp.stack([c_re, -c_im]).reshape(2, 2, n_pairs, 2, n_h, n_p)
    c_tab = c_tab.transpose(1, 4, 2, 0, 3, 5).reshape(2, n_h, n_cols)
    row_group = lax.broadcasted_iota(jnp.int32, (n_rows, n_cols), 0) // n_h
    col = lax.broadcasted_iota(jnp.int32, (n_rows, n_cols), 1)
    col_group = (col // (4 * n_p)) * 2 + (col // n_p) % 2
    on_diagonal = row_group == col_group

    def block_diagonal(tab):
        full = jnp.broadcast_to(tab[:, None], (2, g_half, n_h, n_cols)).reshape(2, n_rows, n_cols)
        return jnp.where(on_diagonal[None], full, 0.0).astype(BF16)

    bc = block_diagonal(b_tab)
    cc = jnp.swapaxes(block_diagonal(c_tab), 1, 2)
    lam = jnp.stack([lb_re.reshape(-1), lb_im.reshape(-1)])
    lam = jnp.broadcast_to(lam[:, None, :], (2, SUBLANES, n_g * n_p))
    return lam, bc, cc


def kernel(x, c, norm1_g, norm2_g, w_ada, b_ada, w_in, lam_re, lam_im, log_dt, b_re, b_im,
           c_re, c_im, d_skip, w_glu, b_glu, conv_w, w_proj_ssm, w_proj_conv, w_out,
           w_ff1, w_ff2, final_g):
    depth = w_in.shape[0]
    n_b, _, d = x.shape
    for l in range(depth):
        mod = _adaln(c, w_ada[l], b_ada[l]).reshape(N_MOD, n_b, 1, d)
        lam, bc, cc = _s5_operators(lam_re[l], lam_im[l], log_dt[l], b_re[l], b_im[l],
                                    c_re[l], c_im[l])
        x = _mixer(x, mod, norm1_g[l][None], w_in[l].astype(BF16), bc, cc, lam,
                   d_skip[l][None], w_glu[l].astype(BF16), b_glu[l][None], conv_w[l],
                   w_proj_ssm[l].astype(BF16), w_proj_conv[l].astype(BF16),
                   w_out[l].astype(BF16))
        x = _mlp(x, mod, norm2_g[l][None], w_ff1[l].astype(BF16), w_ff2[l].astype(BF16),
                 final_g[None], final_norm=(l == depth - 1))
    return x
```

```python
import functools

import jax
import jax.numpy as jnp
from jax import lax
from jax.experimental import pallas as pl
from jax.experimental.pallas import tpu as pltpu

SUBLANES = 8
BF16_ROWS = 16
LANES = 128
MXU_COLS = 256
VMEM_BYTES_V7X = 64 * 1024 * 1024

RMS_EPS = 1e-6
N_MOD = 6
CONV_WIDTH = 3

F32 = jnp.float32
BF16 = jnp.bfloat16


def _tiles(d_model):
    batch_tile = SUBLANES
    time_tile = 64
    mlp_rows = 512
    mlp_slabs = 2
    ff_chunk = 1024
    mixer_vmem = VMEM_BYTES_V7X - 8 * 1024 * 1024
    mlp_vmem = VMEM_BYTES_V7X - 8 * 1024 * 1024
    return batch_tile, time_tile, mlp_rows, mlp_slabs, ff_chunk, mixer_vmem, mlp_vmem


def _rms_normalize(x):
    return x * lax.rsqrt(jnp.mean(x * x, axis=-1, keepdims=True) + RMS_EPS)


def _dot(a, b):
    return jnp.dot(a, b, preferred_element_type=F32)


def _adaln_kernel(c_ref, w_ref, b_ref, o_ref):
    c = c_ref[...]
    c_act = c * jax.nn.sigmoid(c)
    o_ref[0] = _dot(c_act.astype(BF16), w_ref[...].astype(BF16)) + b_ref[...]


def _adaln(c, w_ada, b_ada):
    n_b, d = c.shape
    return pl.pallas_call(
        _adaln_kernel,
        grid=(N_MOD,),
        in_specs=[
            pl.BlockSpec((n_b, d), lambda j: (0, 0)),
            pl.BlockSpec((d, d), lambda j: (0, j)),
            pl.BlockSpec((1, d), lambda j: (0, j)),
        ],
        out_specs=pl.BlockSpec((1, n_b, d), lambda j: (j, 0, 0)),
        out_shape=jax.ShapeDtypeStruct((N_MOD, n_b, d), F32),
        name="adaln",
    )(c, w_ada, b_ada.reshape(1, N_MOD * d))


def _mixer_kernel(x_ref, mod_ref, g_ref, w_in_ref, bc_ref, cc_ref, lam_ref, dskip_ref,
                  wglu_ref, bglu_ref, convw_ref, wps_ref, wpc_ref, wout_ref,
                  o_ref,
                  h_ref, u_ref, utb_ref, bu_ref, xs_ref, ytb_ref, ybt_ref, state_ref, vpad_ref, p_ref,
                  *, time_tile, d_ssm, d_conv, d_model):
    t_n = time_tile
    n_bt = x_ref.shape[0]
    m = n_bt * t_n
    n_slab = d_ssm // LANES
    slabs_per_half = n_slab // 2
    half_cols = bu_ref.shape[1] // 2
    n_pair = bu_ref.shape[1] // (2 * LANES)
    w_p = p_ref.shape[2]
    row_halves = [slice(0, m // 2), slice(m // 2, m)]
    batch_halves = [range(0, n_bt // 2), range(n_bt // 2, n_bt)]

    @pl.when(pl.program_id(1) == 0)
    def _():
        state_ref[...] = jnp.zeros_like(state_ref)
        vpad_ref[:, 0:SUBLANES, :] = jnp.zeros((n_bt, SUBLANES, d_conv), F32)


    def project(first_col, n_cols):
        for k in range(first_col // w_p, (first_col + n_cols) // w_p):
            c0 = d_ssm + k * w_p
            p_ref[k] = _dot(h_ref[...], w_in_ref[:, c0:c0 + w_p])

    def p_cols(first_col, n_cols, rows=slice(None)):
        return jnp.concatenate(
            [p_ref[k, rows, :] for k in range(first_col // w_p, (first_col + n_cols) // w_p)],
            axis=1)

    for rows, batches in zip(row_halves, batch_halves):
        for b in batches:
            hb = _rms_normalize(x_ref[b]) * g_ref[...]
            hb = hb * (1.0 + mod_ref[1, b]) + mod_ref[0, b]
            h_ref[b * t_n:(b + 1) * t_n, :] = hb.astype(BF16)
        u_ref[rows, :] = _dot(h_ref[rows, :], w_in_ref[:, 0:d_ssm])

    c_gate = 3 * d_conv
    n_early_gate = 4 * w_p
    project(0, d_conv)
    project(c_gate, n_early_gate)

    for b in range(n_bt):
        for j in range(n_slab):
            utb_ref[j, pl.ds(b, t_n, stride=n_bt), :] = (
                u_ref[b * t_n:(b + 1) * t_n, j * LANES:(j + 1) * LANES])
    for half in range(2):
        lhs = jnp.concatenate(
            [utb_ref[half * slabs_per_half + j] for j in range(slabs_per_half)], axis=1)
        bu_ref[:, half * half_cols:(half + 1) * half_cols] = _dot(lhs.astype(BF16), bc_ref[half])

    project(d_conv, 2 * d_conv)
    v3 = (p_cols(d_conv, d_conv) * p_cols(2 * d_conv, d_conv)).reshape(n_bt, t_n, d_conv)
    vpad_ref[:, SUBLANES:SUBLANES + t_n, :] = v3
    conv = convw_ref[CONV_WIDTH - 1:CONV_WIDTH, :] * v3
    for k in range(1, CONV_WIDTH):
        shifted = vpad_ref[:, SUBLANES - k:SUBLANES - k + t_n, :]
        conv = conv + convw_ref[CONV_WIDTH - 1 - k:CONV_WIDTH - k, :] * shifted
    vpad_ref[:, 0:SUBLANES, :] = vpad_ref[:, t_n:t_n + SUBLANES, :]

    project(c_gate + n_early_gate, 2 * d_model - n_early_gate - 2 * w_p)
    yc = (p_cols(0, d_conv) * conv.reshape(m, d_conv)).astype(BF16)

    pair_cols = [p * LANES for p in range(n_pair)]
    lam_re = [lam_ref[0, :, c:c + LANES] for c in pair_cols]
    lam_im = [lam_ref[1, :, c:c + LANES] for c in pair_cols]
    x_re = [state_ref[0, :, c:c + LANES] for c in pair_cols]
    x_im = [state_ref[1, :, c:c + LANES] for c in pair_cols]
    steps_per_tile = BF16_ROWS // n_bt
    for t0 in range(0, t_n, steps_per_tile):
        tile_rows = slice(t0 * n_bt, (t0 + steps_per_tile) * n_bt)
        for p, c in enumerate(pair_cols):
            c_re, c_im = 2 * c, 2 * c + LANES
            new_re, new_im = [], []
            for t in range(t0, t0 + steps_per_tile):
                rows = slice(t * n_bt, (t + 1) * n_bt)
                n_re = lam_re[p] * x_re[p] - lam_im[p] * x_im[p] + bu_ref[rows, c_re:c_re + LANES]
                n_im = lam_re[p] * x_im[p] + lam_im[p] * x_re[p] + bu_ref[rows, c_im:c_im + LANES]
                x_re[p], x_im[p] = n_re, n_im
                new_re.append(n_re)
                new_im.append(n_im)
            xs_ref[tile_rows, c_re:c_re + LANES] = jnp.concatenate(new_re, axis=0).astype(BF16)
            xs_ref[tile_rows, c_im:c_im + LANES] = jnp.concatenate(new_im, axis=0).astype(BF16)
    for p, c in enumerate(pair_cols):
        state_ref[0, :, c:c + LANES] = x_re[p]
        state_ref[1, :, c:c + LANES] = x_im[p]

    for half in range(2):
        yh = _dot(xs_ref[:, half * half_cols:(half + 1) * half_cols], cc_ref[half])
        for j in range(slabs_per_half):
            ytb_ref[half * slabs_per_half + j] = yh[:, j * LANES:(j + 1) * LANES]

    project(c_gate + 2 * d_model - 2 * w_p, w_p)

    for b in range(n_bt):
        for j in range(n_slab):
            ybt_ref[b * t_n:(b + 1) * t_n, j * LANES:(j + 1) * LANES] = (
                ytb_ref[j, pl.ds(b, t_n, stride=n_bt), :])
    ys = ybt_ref[...] + dskip_ref[...] * u_ref[...]
    ys = jax.nn.gelu(ys)
    glu = _dot(ys.astype(BF16), wglu_ref[...])

    project(c_gate + 2 * d_model - w_p, w_p)
    conv_out = _dot(yc, wpc_ref[...])

    ys = (ys * jax.nn.sigmoid(glu + bglu_ref[...])).astype(BF16)

    merged = []
    for rows in row_halves:
        ssm_out = _dot(ys[rows, :], wps_ref[...])
        merged.append(jax.nn.sigmoid(p_cols(c_gate, d_model, rows)) * ssm_out
                      + jax.nn.sigmoid(p_cols(c_gate + d_model, d_model, rows)) * conv_out[rows, :])
    for rows, batches, mg in zip(row_halves, batch_halves, merged):
        mixed = _dot(mg.astype(BF16), wout_ref[...])
        r0 = rows.start
        for b in batches:
            o_ref[b] = x_ref[b] + mod_ref[2, b] * mixed[b * t_n - r0:(b + 1) * t_n - r0, :]


def _resident(shape):
    zeros = (0,) * len(shape)
    return pl.BlockSpec(shape, lambda *_: zeros, pipeline_mode=pl.Buffered(1))


def _mixer(x, mod, norm_g, w_in, bc, cc, lam, d_skip, w_glu, b_glu, conv_w, w_ps, w_pc, w_out):
    n_b, n_len, d = x.shape
    d_ssm = w_glu.shape[0]
    d_conv = conv_w.shape[1]
    n_state_cols = lam.shape[2] * 2
    n_bt, t_n, _, _, _, vmem, _ = _tiles(d)
    m = n_bt * t_n
    n_slice = (w_in.shape[1] - d_ssm) // MXU_COLS
    kern = functools.partial(_mixer_kernel, time_tile=t_n, d_ssm=d_ssm, d_conv=d_conv, d_model=d)
    return pl.pallas_call(
        kern,
        grid=(n_b // n_bt, n_len // t_n),
        in_specs=[
            pl.BlockSpec((n_bt, t_n, d), lambda i, t: (i, t, 0)),
            pl.BlockSpec((3, n_bt, 1, d), lambda i, t: (0, i, 0, 0)),
            _resident(norm_g.shape), _resident(w_in.shape), _resident(bc.shape),
            _resident(cc.shape), _resident(lam.shape), _resident(d_skip.shape),
            _resident(w_glu.shape), _resident(b_glu.shape), _resident(conv_w.shape),
            _resident(w_ps.shape), _resident(w_pc.shape), _resident(w_out.shape),
        ],
        out_specs=pl.BlockSpec((n_bt, t_n, d), lambda i, t: (i, t, 0)),
        out_shape=jax.ShapeDtypeStruct(x.shape, F32),
        scratch_shapes=[
            pltpu.VMEM((m, d), BF16),
            pltpu.VMEM((m, d_ssm), F32),
            pltpu.VMEM((d_ssm // LANES, m, LANES), F32),
            pltpu.VMEM((m, n_state_cols), F32),
            pltpu.VMEM((m, n_state_cols), BF16),
            pltpu.VMEM((d_ssm // LANES, m, LANES), F32),
            pltpu.VMEM((m, d_ssm), F32),
            pltpu.VMEM((2, n_bt, n_state_cols // 2), F32),
            pltpu.VMEM((n_bt, t_n + SUBLANES, d_conv), F32),
            pltpu.VMEM((n_slice, m, MXU_COLS), F32),
        ],
        compiler_params=pltpu.CompilerParams(
            dimension_semantics=("arbitrary", "arbitrary"), vmem_limit_bytes=vmem),
        name="mixer",
    )(x, mod, norm_g, w_in, bc, cc, lam, d_skip, w_glu, b_glu, conv_w, w_ps, w_pc, w_out)


def _mlp_kernel(x_ref, mod_ref, g_ref, w1_ref, w2_ref, fg_ref, o_ref,
                *, slab_rows, ff_chunk, final_norm):
    d_ff = w1_ref.shape[1]
    chunks = list(range(0, d_ff, ff_chunk))
    slabs = list(range(0, x_ref.shape[1], slab_rows))
    h, f, act = {}, {}, {}

    half = slab_rows // 2

    def normed(r0, n_rows):
        hn = _rms_normalize(x_ref[0, r0:r0 + n_rows, :]) * g_ref[...]
        return (hn * (1.0 + mod_ref[1, 0]) + mod_ref[0, 0]).astype(BF16)

    def up(r0, c0):
        a = jnp.maximum(_dot(h[r0], w1_ref[:, c0:c0 + ff_chunk]), 0.0)
        act[r0, c0] = (a * a).astype(BF16)

    def down(r0, c0, rows=slice(None)):
        part = _dot(act[r0, c0][rows, :], w2_ref[c0:c0 + ff_chunk, :])
        return part if r0 not in f else f[r0][rows, :] + part

    def finish(r0, f_rows, row_lo, n_rows):
        x2 = x_ref[0, r0 + row_lo:r0 + row_lo + n_rows, :] + mod_ref[2, 0] * f_rows
        o_ref[0, r0 + row_lo:r0 + row_lo + n_rows, :] = (
            _rms_normalize(x2) * fg_ref[...] if final_norm else x2)

    steps = [(r0, c0) for r0 in slabs for c0 in chunks]
    first = slabs[0]
    h_top = normed(first, half)
    a_top = jnp.maximum(_dot(h_top, w1_ref[:, 0:ff_chunk]), 0.0)
    h_bot = normed(first + half, half)
    a_bot = jnp.maximum(_dot(h_bot, w1_ref[:, 0:ff_chunk]), 0.0)
    h[first] = jnp.concatenate([h_top, h_bot], axis=0)
    act[steps[0]] = jnp.concatenate([(a_top * a_top).astype(BF16), (a_bot * a_bot).astype(BF16)],
                                    axis=0)
    for r0 in slabs[1:]:
        h[r0] = normed(r0, slab_rows)
    for prev, cur in zip(steps, steps[1:]):
        up(*cur)
        f[prev[0]] = down(*prev)
        if prev[1] == chunks[-1]:
            finish(prev[0], f[prev[0]], 0, slab_rows)
    last = steps[-1]
    for row_lo in (0, half):
        finish(last[0], down(*last, rows=slice(row_lo, row_lo + half)), row_lo, half)


def _mlp(x, mod, norm_g, w1, w2, final_g, final_norm):
    n_b, n_len, d = x.shape
    _, _, slab_rows, n_slabs, ff_chunk, _, vmem = _tiles(d)
    rows = slab_rows * n_slabs
    return pl.pallas_call(
        functools.partial(_mlp_kernel, slab_rows=slab_rows, ff_chunk=ff_chunk,
                          final_norm=final_norm),
        grid=(n_b, n_len // rows),
        in_specs=[
            pl.BlockSpec((1, rows, d), lambda b, i: (b, i, 0)),
            pl.BlockSpec((3, 1, 1, d), lambda b, i: (1, b, 0, 0)),
            _resident(norm_g.shape), _resident(w1.shape), _resident(w2.shape),
            _resident(final_g.shape),
        ],
        out_specs=pl.BlockSpec((1, rows, d), lambda b, i: (b, i, 0)),
        out_shape=jax.ShapeDtypeStruct(x.shape, F32),
        compiler_params=pltpu.CompilerParams(
            dimension_semantics=("arbitrary", "arbitrary"), vmem_limit_bytes=vmem),
        name="mlp",
    )(x, mod, norm_g, w1, w2, final_g)


def _s5_operators(lam_re, lam_im, log_dt, b_re, b_im, c_re, c_im):
    n_g, n_p = lam_re.shape
    n_h = b_re.shape[-1]
    dt = jnp.exp(log_dt)[:, None]
    mag = jnp.exp(lam_re * dt)
    lb_re = mag * jnp.cos(lam_im * dt)
    lb_im = mag * jnp.sin(lam_im * dt)
    den = lam_re * lam_re + lam_im * lam_im
    k_re = ((lb_re - 1.0) * lam_re + lb_im * lam_im) / den
    k_im = (lb_im * lam_re - (lb_re - 1.0) * lam_im) / den
    bb_re = k_re[..., None] * b_re - k_im[..., None] * b_im
    bb_im = k_re[..., None] * b_im + k_im[..., None] * b_re

    g_half = n_g // 2
    n_pairs = g_half // 2
    n_rows = g_half * n_h
    n_cols = g_half * 2 * n_p
    b_tab = jnp.stack([bb_re, bb_im]).reshape(2, 2, n_pairs, 2, n_p, n_h)
    b_tab = b_tab.transpose(1, 5, 2, 0, 3, 4).reshape(2, n_h, n_cols)
    c_tab = jnp.stack([c_re, -c_im]).reshape(2, 2, n_pairs, 2, n_h, n_p)
    c_tab = c_tab.transpose(1, 4, 2, 0, 3, 5).reshape(2, n_h, n_cols)
    row_group = lax.broadcasted_iota(jnp.int32, (n_rows, n_cols), 0) // n_h
    col = lax.broadcasted_iota(jnp.int32, (n_rows, n_cols), 1)
    col_group = (col // (4 * n_p)) * 2 + (col // n_p) % 2
    on_diagonal = row_group == col_group

    def block_diagonal(tab):
        full = jnp.broadcast_to(tab[:, None], (2, g_half, n_h, n_cols)).reshape(2, n_rows, n_cols)
        return jnp.where(on_diagonal[None], full, 0.0).astype(BF16)

    bc = block_diagonal(b_tab)
    cc = jnp.swapaxes(block_diagonal(c_tab), 1, 2)
    lam = jnp.stack([lb_re.reshape(-1), lb_im.reshape(-1)])
    lam = jnp.broadcast_to(lam[:, None, :], (2, SUBLANES, n_g * n_p))
    return lam, bc, cc


def kernel(x, c, norm1_g, norm2_g, w_ada, b_ada, w_in, lam_re, lam_im, log_dt, b_re, b_im,
           c_re, c_im, d_skip, w_glu, b_glu, conv_w, w_proj_ssm, w_proj_conv, w_out,
           w_ff1, w_ff2, final_g):
    depth = w_in.shape[0]
    n_b, _, d = x.shape
    for l in range(depth):
        mod = _adaln(c, w_ada[l], b_ada[l]).reshape(N_MOD, n_b, 1, d)
        lam, bc, cc = _s5_operators(lam_re[l], lam_im[l], log_dt[l], b_re[l], b_im[l],
                                    c_re[l], c_im[l])
        x = _mixer(x, mod, norm1_g[l][None], w_in[l].astype(BF16), bc, cc, lam,
                   d_skip[l][None], w_glu[l].astype(BF16), b_glu[l][None], conv_w[l],
                   w_proj_ssm[l].astype(BF16), w_proj_conv[l].astype(BF16),
                   w_out[l].astype(BF16))
        x = _mlp(x, mod, norm2_g[l][None], w_ff1[l].astype(BF16), w_ff2[l].astype(BF16),
                 final_g[None], final_norm=(l == depth - 1))
    return x
```

```python
import functools

import jax
import jax.numpy as jnp
from jax import lax
from jax.experimental import pallas as pl
from jax.experimental.pallas import tpu as pltpu

SUBLANES = 8
BF16_ROWS = 16
LANES = 128
MXU_COLS = 256
VMEM_BYTES_V7X = 64 * 1024 * 1024

RMS_EPS = 1e-6
N_MOD = 6
CONV_WIDTH = 3

F32 = jnp.float32
BF16 = jnp.bfloat16


def _tiles(d_model):
    batch_tile = SUBLANES
    time_tile = 64
    mlp_rows = 512
    mlp_slabs = 2
    ff_chunk = 1024
    mixer_vmem = VMEM_BYTES_V7X - 8 * 1024 * 1024
    mlp_vmem = VMEM_BYTES_V7X - 8 * 1024 * 1024
    return batch_tile, time_tile, mlp_rows, mlp_slabs, ff_chunk, mixer_vmem, mlp_vmem


def _rms_normalize(x):
    return x * lax.rsqrt(jnp.mean(x * x, axis=-1, keepdims=True) + RMS_EPS)


def _dot(a, b):
    return jnp.dot(a, b, preferred_element_type=F32)


def _adaln_kernel(c_ref, w_ref, b_ref, o_ref):
    c = c_ref[...]
    c_act = c * jax.nn.sigmoid(c)
    o_ref[0] = _dot(c_act.astype(BF16), w_ref[...].astype(BF16)) + b_ref[...]


def _adaln(c, w_ada, b_ada):
    n_b, d = c.shape
    return pl.pallas_call(
        _adaln_kernel,
        grid=(N_MOD,),
        in_specs=[
            pl.BlockSpec((n_b, d), lambda j: (0, 0)),
            pl.BlockSpec((d, d), lambda j: (0, j)),
            pl.BlockSpec((1, d), lambda j: (0, j)),
        ],
        out_specs=pl.BlockSpec((1, n_b, d), lambda j: (j, 0, 0)),
        out_shape=jax.ShapeDtypeStruct((N_MOD, n_b, d), F32),
        name="adaln",
    )(c, w_ada, b_ada.reshape(1, N_MOD * d))


def _mixer_kernel(x_ref, mod_ref, g_ref, w_in_ref, bc_ref, cc_ref, lam_ref, dskip_ref,
                  wglu_ref, bglu_ref, convw_ref, wps_ref, wpc_ref, wout_ref,
                  o_ref,
                  h_ref, u_ref, utb_ref, bu_ref, xs_ref, ytb_ref, ybt_ref, state_ref, vpad_ref, p_ref,
                  *, time_tile, d_ssm, d_conv, d_model):
    t_n = time_tile
    n_bt = x_ref.shape[0]
    m = n_bt * t_n
    n_slab = d_ssm // LANES
    slabs_per_half = n_slab // 2
    half_cols = bu_ref.shape[1] // 2
    n_pair = bu_ref.shape[1] // (2 * LANES)
    w_p = p_ref.shape[2]
    row_halves = [slice(0, m // 2), slice(m // 2, m)]
    batch_halves = [range(0, n_bt // 2), range(n_bt // 2, n_bt)]

    @pl.when(pl.program_id(1) == 0)
    def _():
        state_ref[...] = jnp.zeros_like(state_ref)
        vpad_ref[:, 0:SUBLANES, :] = jnp.zeros((n_bt, SUBLANES, d_conv), F32)


    def project(first_col, n_cols):
        for k in range(first_col // w_p, (first_col + n_cols) // w_p):
            c0 = d_ssm + k * w_p
            p_ref[k] = _dot(h_ref[...], w_in_ref[:, c0:c0 + w_p])

    def p_cols(first_col, n_cols, rows=slice(None)):
        return jnp.concatenate(
            [p_ref[k, rows, :] for k in range(first_col // w_p, (first_col + n_cols) // w_p)],
            axis=1)

    for rows, batches in zip(row_halves, batch_halves):
        for b in batches:
            gain = g_ref[...] * (1.0 + mod_ref[1, b])
            hb = _rms_normalize(x_ref[b]) * gain + mod_ref[0, b]
            h_ref[b * t_n:(b + 1) * t_n, :] = hb.astype(BF16)
        u_ref[rows, :] = _dot(h_ref[rows, :], w_in_ref[:, 0:d_ssm])

    c_gate = 3 * d_conv
    n_early_gate = 4 * w_p
    project(0, d_conv)
    project(c_gate, n_early_gate)

    for b in range(n_bt):
        for j in range(n_slab):
            utb_ref[j, pl.ds(b, t_n, stride=n_bt), :] = (
                u_ref[b * t_n:(b + 1) * t_n, j * LANES:(j + 1) * LANES])
    for half in range(2):
        lhs = jnp.concatenate(
            [utb_ref[half * slabs_per_half + j] for j in range(slabs_per_half)], axis=1)
        bu_ref[:, half * half_cols:(half + 1) * half_cols] = _dot(lhs.astype(BF16), bc_ref[half])

    project(d_conv, 2 * d_conv)
    v3 = (p_cols(d_conv, d_conv) * p_cols(2 * d_conv, d_conv)).reshape(n_bt, t_n, d_conv)
    vpad_ref[:, SUBLANES:SUBLANES + t_n, :] = v3
    conv = convw_ref[CONV_WIDTH - 1:CONV_WIDTH, :] * v3
    for k in range(1, CONV_WIDTH):
        shifted = vpad_ref[:, SUBLANES - k:SUBLANES - k + t_n, :]
        conv = conv + convw_ref[CONV_WIDTH - 1 - k:CONV_WIDTH - k, :] * shifted
    vpad_ref[:, 0:SUBLANES, :] = vpad_ref[:, t_n:t_n + SUBLANES, :]

    project(c_gate + n_early_gate, 2 * d_model - n_early_gate - 2 * w_p)
    yc = (p_cols(0, d_conv) * conv.reshape(m, d_conv)).astype(BF16)

    pair_cols = [p * LANES for p in range(n_pair)]
    x_re = [state_ref[0, :, c:c + LANES] for c in pair_cols]
    x_im = [state_ref[1, :, c:c + LANES] for c in pair_cols]
    steps_per_tile = BF16_ROWS // n_bt
    for t0 in range(0, t_n, steps_per_tile):
        tile_rows = slice(t0 * n_bt, (t0 + steps_per_tile) * n_bt)
        for p, c in enumerate(pair_cols):
            c_re, c_im = 2 * c, 2 * c + LANES
            new_re, new_im = [], []
            for t in range(t0, t0 + steps_per_tile):
                rows = slice(t * n_bt, (t + 1) * n_bt)
                l_re = lam_ref[0, :, c:c + LANES]
                l_im = lam_ref[1, :, c:c + LANES]
                n_re = l_re * x_re[p] - l_im * x_im[p] + bu_ref[rows, c_re:c_re + LANES]
                n_im = l_re * x_im[p] + l_im * x_re[p] + bu_ref[rows, c_im:c_im + LANES]
                x_re[p], x_im[p] = n_re, n_im
                new_re.append(n_re)
                new_im.append(n_im)
            xs_ref[tile_rows, c_re:c_re + LANES] = jnp.concatenate(new_re, axis=0).astype(BF16)
            xs_ref[tile_rows, c_im:c_im + LANES] = jnp.concatenate(new_im, axis=0).astype(BF16)
    for p, c in enumerate(pair_cols):
        state_ref[0, :, c:c + LANES] = x_re[p]
        state_ref[1, :, c:c + LANES] = x_im[p]

    for half in range(2):
        yh = _dot(xs_ref[:, half * half_cols:(half + 1) * half_cols], cc_ref[half])
        for j in range(slabs_per_half):
            ytb_ref[half * slabs_per_half + j] = yh[:, j * LANES:(j + 1) * LANES]

    project(c_gate + 2 * d_model - 2 * w_p, w_p)

    for b in range(n_bt):
        for j in range(n_slab):
            ybt_ref[b * t_n:(b + 1) * t_n, j * LANES:(j + 1) * LANES] = (
                ytb_ref[j, pl.ds(b, t_n, stride=n_bt), :])
    ys = ybt_ref[...] + dskip_ref[...] * u_ref[...]
    ys = jax.nn.gelu(ys)
    glu = _dot(ys.astype(BF16), wglu_ref[...])

    project(c_gate + 2 * d_model - w_p, w_p)
    conv_out = _dot(yc, wpc_ref[...])

    ys = (ys * jax.nn.sigmoid(glu + bglu_ref[...])).astype(BF16)

    merged = []
    for rows in row_halves:
        ssm_out = _dot(ys[rows, :], wps_ref[...])
        merged.append(jax.nn.sigmoid(p_cols(c_gate, d_model, rows)) * ssm_out
                      + jax.nn.sigmoid(p_cols(c_gate + d_model, d_model, rows)) * conv_out[rows, :])
    for rows, batches, mg in zip(row_halves, batch_halves, merged):
        mixed = _dot(mg.astype(BF16), wout_ref[...])
        r0 = rows.start
        for b in batches:
            o_ref[b] = x_ref[b] + mod_ref[2, b] * mixed[b * t_n - r0:(b + 1) * t_n - r0, :]


def _resident(shape):
    zeros = (0,) * len(shape)
    return pl.BlockSpec(shape, lambda *_: zeros, pipeline_mode=pl.Buffered(1))


def _mixer(x, mod, norm_g, w_in, bc, cc, lam, d_skip, w_glu, b_glu, conv_w, w_ps, w_pc, w_out):
    n_b, n_len, d = x.shape
    d_ssm = w_glu.shape[0]
    d_conv = conv_w.shape[1]
    n_state_cols = lam.shape[2] * 2
    n_bt, t_n, _, _, _, vmem, _ = _tiles(d)
    m = n_bt * t_n
    n_slice = (w_in.shape[1] - d_ssm) // MXU_COLS
    assert n_b % n_bt == 0 and n_len % t_n == 0 and t_n % BF16_ROWS == 0
    assert d_ssm % (2 * MXU_COLS) == 0 and d_conv % MXU_COLS == 0 and d % MXU_COLS == 0
    assert w_in.shape[1] == d_ssm + 3 * d_conv + 2 * d and n_state_cols % (4 * LANES) == 0
    kern = functools.partial(_mixer_kernel, time_tile=t_n, d_ssm=d_ssm, d_conv=d_conv, d_model=d)
    return pl.pallas_call(
        kern,
        grid=(n_b // n_bt, n_len // t_n),
        in_specs=[
            pl.BlockSpec((n_bt, t_n, d), lambda i, t: (i, t, 0)),
            pl.BlockSpec((3, n_bt, 1, d), lambda i, t: (0, i, 0, 0)),
            _resident(norm_g.shape), _resident(w_in.shape), _resident(bc.shape),
            _resident(cc.shape), _resident(lam.shape), _resident(d_skip.shape),
            _resident(w_glu.shape), _resident(b_glu.shape), _resident(conv_w.shape),
            _resident(w_ps.shape), _resident(w_pc.shape), _resident(w_out.shape),
        ],
        out_specs=pl.BlockSpec((n_bt, t_n, d), lambda i, t: (i, t, 0)),
        out_shape=jax.ShapeDtypeStruct(x.shape, F32),
        scratch_shapes=[
            pltpu.VMEM((m, d), BF16),
            pltpu.VMEM((m, d_ssm), F32),
            pltpu.VMEM((d_ssm // LANES, m, LANES), F32),
            pltpu.VMEM((m, n_state_cols), F32),
            pltpu.VMEM((m, n_state_cols), BF16),
            pltpu.VMEM((d_ssm // LANES, m, LANES), F32),
            pltpu.VMEM((m, d_ssm), F32),
            pltpu.VMEM((2, n_bt, n_state_cols // 2), F32),
            pltpu.VMEM((n_bt, t_n + SUBLANES, d_conv), F32),
            pltpu.VMEM((n_slice, m, MXU_COLS), F32),
        ],
        compiler_params=pltpu.CompilerParams(
            dimension_semantics=("arbitrary", "arbitrary"), vmem_limit_bytes=vmem),
        name="mixer",
    )(x, mod, norm_g, w_in, bc, cc, lam, d_skip, w_glu, b_glu, conv_w, w_ps, w_pc, w_out)


def _mlp_kernel(x_ref, mod_ref, g_ref, w1_ref, w2_ref, fg_ref, o_ref,
                *, slab_rows, ff_chunk, final_norm):
    d_ff = w1_ref.shape[1]
    chunks = list(range(0, d_ff, ff_chunk))
    slabs = list(range(0, x_ref.shape[1], slab_rows))
    h, f, act = {}, {}, {}

    half = slab_rows // 2

    gain = g_ref[...] * (1.0 + mod_ref[1, 0])

    def normed(r0, n_rows):
        return (_rms_normalize(x_ref[0, r0:r0 + n_rows, :]) * gain + mod_ref[0, 0]).astype(BF16)

    def up(r0, c0):
        a = jnp.maximum(_dot(h[r0], w1_ref[:, c0:c0 + ff_chunk]), 0.0)
        act[r0, c0] = (a * a).astype(BF16)

    def down(r0, c0, rows=slice(None)):
        part = _dot(act[r0, c0][rows, :], w2_ref[c0:c0 + ff_chunk, :])
        return part if r0 not in f else f[r0][rows, :] + part

    def finish(r0, f_rows, row_lo, n_rows):
        x2 = x_ref[0, r0 + row_lo:r0 + row_lo + n_rows, :] + mod_ref[2, 0] * f_rows
        o_ref[0, r0 + row_lo:r0 + row_lo + n_rows, :] = (
            _rms_normalize(x2) * fg_ref[...] if final_norm else x2)

    steps = [(r0, c0) for r0 in slabs for c0 in chunks]
    first = slabs[0]
    h_top = normed(first, half)
    a_top = jnp.maximum(_dot(h_top, w1_ref[:, 0:ff_chunk]), 0.0)
    h_bot = normed(first + half, half)
    a_bot = jnp.maximum(_dot(h_bot, w1_ref[:, 0:ff_chunk]), 0.0)
    h[first] = jnp.concatenate([h_top, h_bot], axis=0)
    act[steps[0]] = jnp.concatenate([(a_top * a_top).astype(BF16), (a_bot * a_bot).astype(BF16)],
                                    axis=0)
    for r0 in slabs[1:]:
        h[r0] = normed(r0, slab_rows)
    for prev, cur in zip(steps, steps[1:]):
        up(*cur)
        f[prev[0]] = down(*prev)
        if prev[1] == chunks[-1]:
            finish(prev[0], f[prev[0]], 0, slab_rows)
    last = steps[-1]
    for row_lo in (0, half):
        finish(last[0], down(*last, rows=slice(row_lo, row_lo + half)), row_lo, half)


def _mlp(x, mod, norm_g, w1, w2, final_g, final_norm):
    n_b, n_len, d = x.shape
    _, _, slab_rows, n_slabs, ff_chunk, _, vmem = _tiles(d)
    rows = slab_rows * n_slabs
    assert n_len % rows == 0 and w1.shape[1] % ff_chunk == 0 and slab_rows % (2 * BF16_ROWS) == 0
    return pl.pallas_call(
        functools.partial(_mlp_kernel, slab_rows=slab_rows, ff_chunk=ff_chunk,
                          final_norm=final_norm),
        grid=(n_b, n_len // rows),
        in_specs=[
            pl.BlockSpec((1, rows, d), lambda b, i: (b, i, 0)),
            pl.BlockSpec((3, 1, 1, d), lambda b, i: (1, b, 0, 0)),
            _resident(norm_g.shape), _resident(w1.shape), _resident(w2.shape),
            _resident(final_g.shape),
        ],
        out_specs=pl.BlockSpec((1, rows, d), lambda b, i: (b, i, 0)),
        out_shape=jax.ShapeDtypeStruct(x.shape, F32),
        compiler_params=pltpu.CompilerParams(
            dimension_semantics=("arbitrary", "arbitrary"), vmem_limit_bytes=vmem),
        name="mlp",
    )(x, mod, norm_g, w1, w2, final_g)


def _s5_operators(lam_re, lam_im, log_dt, b_re, b_im, c_re, c_im):
    n_g, n_p = lam_re.shape
    n_h = b_re.shape[-1]
    dt = jnp.exp(log_dt)[:, None]
    mag = jnp.exp(lam_re * dt)
    lb_re = mag * jnp.cos(lam_im * dt)
    lb_im = mag * jnp.sin(lam_im * dt)
    den = lam_re * lam_re + lam_im * lam_im
    k_re = ((lb_re - 1.0) * lam_re + lb_im * lam_im) / den
    k_im = (lb_im * lam_re - (lb_re - 1.0) * lam_im) / den
    bb_re = k_re[..., None] * b_re - k_im[..., None] * b_im
    bb_im = k_re[..., None] * b_im + k_im[..., None] * b_re

    g_half = n_g // 2
    n_pairs = g_half // 2
    n_rows = g_half * n_h
    n_cols = g_half * 2 * n_p
    b_tab = jnp.stack([bb_re, bb_im]).reshape(2, 2, n_pairs, 2, n_p, n_h)
    b_tab = b_tab.transpose(1, 5, 2, 0, 3, 4).reshape(2, n_h, n_cols)
    c_tab = jnp.stack([c_re, -c_im]).reshape(2, 2, n_pairs, 2, n_h, n_p)
    c_tab = c_tab.transpose(1, 4, 2, 0, 3, 5).reshape(2, n_h, n_cols)
    row_group = lax.broadcasted_iota(jnp.int32, (n_rows, n_cols), 0) // n_h
    col = lax.broadcasted_iota(jnp.int32, (n_rows, n_cols), 1)
    col_group = (col // (4 * n_p)) * 2 + (col // n_p) % 2
    on_diagonal = row_group == col_group

    def block_diagonal(tab):
        full = jnp.broadcast_to(tab[:, None], (2, g_half, n_h, n_cols)).reshape(2, n_rows, n_cols)
        return jnp.where(on_diagonal[None], full, 0.0).astype(BF16)

    bc = block_diagonal(b_tab)
    cc = jnp.swapaxes(block_diagonal(c_tab), 1, 2)
    lam = jnp.stack([lb_re.reshape(-1), lb_im.reshape(-1)])
    lam = jnp.broadcast_to(lam[:, None, :], (2, SUBLANES, n_g * n_p))
    return lam, bc, cc


def kernel(x, c, norm1_g, norm2_g, w_ada, b_ada, w_in, lam_re, lam_im, log_dt, b_re, b_im,
           c_re, c_im, d_skip, w_glu, b_glu, conv_w, w_proj_ssm, w_proj_conv, w_out,
           w_ff1, w_ff2, final_g):
    depth = w_in.shape[0]
    n_b, _, d = x.shape
    for l in range(depth):
        mod = _adaln(c, w_ada[l], b_ada[l]).reshape(N_MOD, n_b, 1, d)
        lam, bc, cc = _s5_operators(lam_re[l], lam_im[l], log_dt[l], b_re[l], b_im[l],
                                    c_re[l], c_im[l])
        x = _mixer(x, mod, norm1_g[l][None], w_in[l].astype(BF16), bc, cc, lam,
                   d_skip[l][None], w_glu[l].astype(BF16), b_glu[l][None], conv_w[l],
                   w_proj_ssm[l].astype(BF16), w_proj_conv[l].astype(BF16),
                   w_out[l].astype(BF16))
        x = _mlp(x, mod, norm2_g[l][None], w_ff1[l].astype(BF16), w_ff2[l].astype(BF16),
                 final_g[None], final_norm=(l == depth - 1))
    return x
```

```python
import functools

import jax
import jax.numpy as jnp
from jax import lax
from jax.experimental import pallas as pl
from jax.experimental.pallas import tpu as pltpu

SUBLANES = 8
BF16_ROWS = 16
LANES = 128
MXU_COLS = 256
VMEM_BYTES_V7X = 64 * 1024 * 1024

RMS_EPS = 1e-6
N_MOD = 6
CONV_WIDTH = 3

F32 = jnp.float32
BF16 = jnp.bfloat16


def _tiles(d_model):
    batch_tile = SUBLANES
    time_tile = 64
    mlp_rows = 512
    mlp_slabs = 2
    ff_chunk = 1024
    mixer_vmem = VMEM_BYTES_V7X - 8 * 1024 * 1024
    mlp_vmem = VMEM_BYTES_V7X - 8 * 1024 * 1024
    return batch_tile, time_tile, mlp_rows, mlp_slabs, ff_chunk, mixer_vmem, mlp_vmem


def _rms_normalize(x):
    return x * lax.rsqrt(jnp.mean(x * x, axis=-1, keepdims=True) + RMS_EPS)


def _dot(a, b):
    return jnp.dot(a, b, preferred_element_type=F32)


def _adaln_kernel(c_ref, w_ref, b_ref, o_ref):
    c = c_ref[...]
    c_act = c * jax.nn.sigmoid(c)
    o_ref[0] = _dot(c_act.astype(BF16), w_ref[...].astype(BF16)) + b_ref[...]


def _adaln(c, w_ada, b_ada):
    n_b, d = c.shape
    return pl.pallas_call(
        _adaln_kernel,
        grid=(N_MOD,),
        in_specs=[
            pl.BlockSpec((n_b, d), lambda j: (0, 0)),
            pl.BlockSpec((d, d), lambda j: (0, j)),
            pl.BlockSpec((1, d), lambda j: (0, j)),
        ],
        out_specs=pl.BlockSpec((1, n_b, d), lambda j: (j, 0, 0)),
        out_shape=jax.ShapeDtypeStruct((N_MOD, n_b, d), F32),
        name="adaln",
    )(c, w_ada, b_ada.reshape(1, N_MOD * d))


def _mixer_kernel(x_ref, mod_ref, g_ref, w_in_ref, bc_ref, cc_ref, lam_ref, dskip_ref,
                  wglu_ref, bglu_ref, convw_ref, wps_ref, wpc_ref, wout_ref,
                  o_ref,
                  h_ref, u_ref, utb_ref, bu_ref, xs_ref, ytb_ref, ybt_ref, state_ref, vpad_ref, p_ref,
                  *, time_tile, d_ssm, d_conv, d_model):
    t_n = time_tile
    n_bt = x_ref.shape[0]
    m = n_bt * t_n
    n_slab = d_ssm // LANES
    slabs_per_half = n_slab // 2
    half_cols = bu_ref.shape[1] // 2
    n_pair = bu_ref.shape[1] // (2 * LANES)
    w_p = p_ref.shape[2]
    row_halves = [slice(0, m // 2), slice(m // 2, m)]
    batch_halves = [range(0, n_bt // 2), range(n_bt // 2, n_bt)]

    @pl.when(pl.program_id(1) == 0)
    def _():
        state_ref[...] = jnp.zeros_like(state_ref)
        vpad_ref[:, 0:SUBLANES, :] = jnp.zeros((n_bt, SUBLANES, d_conv), F32)


    def project(first_col, n_cols):
        for k in range(first_col // w_p, (first_col + n_cols) // w_p):
            c0 = d_ssm + k * w_p
            p_ref[k] = _dot(h_ref[...], w_in_ref[:, c0:c0 + w_p])

    def p_cols(first_col, n_cols, rows=slice(None)):
        return jnp.concatenate(
            [p_ref[k, rows, :] for k in range(first_col // w_p, (first_col + n_cols) // w_p)],
            axis=1)

    for rows, batches in zip(row_halves, batch_halves):
        for b in batches:
            gain = g_ref[...] * (1.0 + mod_ref[1, b])
            hb = _rms_normalize(x_ref[b]) * gain + mod_ref[0, b]
            h_ref[b * t_n:(b + 1) * t_n, :] = hb.astype(BF16)
        u_ref[rows, :] = _dot(h_ref[rows, :], w_in_ref[:, 0:d_ssm])

    c_gate = 3 * d_conv
    n_early_gate = 4 * w_p
    project(0, d_conv)
    project(c_gate, n_early_gate)

    for b in range(n_bt):
        for j in range(n_slab):
            utb_ref[j, pl.ds(b, t_n, stride=n_bt), :] = (
                u_ref[b * t_n:(b + 1) * t_n, j * LANES:(j + 1) * LANES])
    for half in range(2):
        lhs = jnp.concatenate(
            [utb_ref[half * slabs_per_half + j] for j in range(slabs_per_half)], axis=1)
        bu_ref[:, half * half_cols:(half + 1) * half_cols] = _dot(lhs.astype(BF16), bc_ref[half])

    project(d_conv, 2 * d_conv)
    v3 = (p_cols(d_conv, d_conv) * p_cols(2 * d_conv, d_conv)).reshape(n_bt, t_n, d_conv)
    vpad_ref[:, SUBLANES:SUBLANES + t_n, :] = v3
    conv = convw_ref[CONV_WIDTH - 1:CONV_WIDTH, :] * v3
    for k in range(1, CONV_WIDTH):
        shifted = vpad_ref[:, SUBLANES - k:SUBLANES - k + t_n, :]
        conv = conv + convw_ref[CONV_WIDTH - 1 - k:CONV_WIDTH - k, :] * shifted
    vpad_ref[:, 0:SUBLANES, :] = vpad_ref[:, t_n:t_n + SUBLANES, :]

    yc = (p_cols(0, d_conv) * conv.reshape(m, d_conv)).astype(BF16)

    pair_cols = [p * LANES for p in range(n_pair)]
    x_re = [state_ref[0, :, c:c + LANES] for c in pair_cols]
    x_im = [state_ref[1, :, c:c + LANES] for c in pair_cols]
    steps_per_tile = BF16_ROWS // n_bt
    for t0 in range(0, t_n, steps_per_tile):
        tile_rows = slice(t0 * n_bt, (t0 + steps_per_tile) * n_bt)
        for p, c in enumerate(pair_cols):
            c_re, c_im = 2 * c, 2 * c + LANES
            new_re, new_im = [], []
            for t in range(t0, t0 + steps_per_tile):
                rows = slice(t * n_bt, (t + 1) * n_bt)
                l_re = lam_ref[0, :, c:c + LANES]
                l_im = lam_ref[1, :, c:c + LANES]
                n_re = l_re * x_re[p] - l_im * x_im[p] + bu_ref[rows, c_re:c_re + LANES]
                n_im = l_re * x_im[p] + l_im * x_re[p] + bu_ref[rows, c_im:c_im + LANES]
                x_re[p], x_im[p] = n_re, n_im
                new_re.append(n_re)
                new_im.append(n_im)
            xs_ref[tile_rows, c_re:c_re + LANES] = jnp.concatenate(new_re, axis=0).astype(BF16)
            xs_ref[tile_rows, c_im:c_im + LANES] = jnp.concatenate(new_im, axis=0).astype(BF16)
    for p, c in enumerate(pair_cols):
        state_ref[0, :, c:c + LANES] = x_re[p]
        state_ref[1, :, c:c + LANES] = x_im[p]

    for half in range(2):
        yh = _dot(xs_ref[:, half * half_cols:(half + 1) * half_cols], cc_ref[half])
        for j in range(slabs_per_half):
            ytb_ref[half * slabs_per_half + j] = yh[:, j * LANES:(j + 1) * LANES]
        if half == 0:
            project(c_gate + n_early_gate, 2 * d_model - n_early_gate - 2 * w_p)

    project(c_gate + 2 * d_model - 2 * w_p, w_p)

    for b in range(n_bt):
        for j in range(n_slab):
            ybt_ref[b * t_n:(b + 1) * t_n, j * LANES:(j + 1) * LANES] = (
                ytb_ref[j, pl.ds(b, t_n, stride=n_bt), :])
    ys = ybt_ref[...] + dskip_ref[...] * u_ref[...]
    ys = jax.nn.gelu(ys)
    glu = _dot(ys.astype(BF16), wglu_ref[...])

    project(c_gate + 2 * d_model - w_p, w_p)
    conv_out = _dot(yc, wpc_ref[...])

    ys = (ys * jax.nn.sigmoid(glu + bglu_ref[...])).astype(BF16)

    merged = []
    for rows in row_halves:
        ssm_out = _dot(ys[rows, :], wps_ref[...])
        merged.append(jax.nn.sigmoid(p_cols(c_gate, d_model, rows)) * ssm_out
                      + jax.nn.sigmoid(p_cols(c_gate + d_model, d_model, rows)) * conv_out[rows, :])
    for rows, batches, mg in zip(row_halves, batch_halves, merged):
        mixed = _dot(mg.astype(BF16), wout_ref[...])
        r0 = rows.start
        for b in batches:
            o_ref[b] = x_ref[b] + mod_ref[2, b] * mixed[b * t_n - r0:(b + 1) * t_n - r0, :]


def _resident(shape):
    zeros = (0,) * len(shape)
    return pl.BlockSpec(shape, lambda *_: zeros, pipeline_mode=pl.Buffered(1))


def _mixer(x, mod, norm_g, w_in, bc, cc, lam, d_skip, w_glu, b_glu, conv_w, w_ps, w_pc, w_out):
    n_b, n_len, d = x.shape
    d_ssm = w_glu.shape[0]
    d_conv = conv_w.shape[1]
    n_state_cols = lam.shape[2] * 2
    n_bt, t_n, _, _, _, vmem, _ = _tiles(d)
    m = n_bt * t_n
    n_slice = (w_in.shape[1] - d_ssm) // MXU_COLS
    assert n_b % n_bt == 0 and n_len % t_n == 0 and t_n % BF16_ROWS == 0
    assert d_ssm % (2 * MXU_COLS) == 0 and d_conv % MXU_COLS == 0 and d % MXU_COLS == 0
    assert w_in.shape[1] == d_ssm + 3 * d_conv + 2 * d and n_state_cols % (4 * LANES) == 0
    kern = functools.partial(_mixer_kernel, time_tile=t_n, d_ssm=d_ssm, d_conv=d_conv, d_model=d)
    return pl.pallas_call(
        kern,
        grid=(n_b // n_bt, n_len // t_n),
        in_specs=[
            pl.BlockSpec((n_bt, t_n, d), lambda i, t: (i, t, 0)),
            pl.BlockSpec((3, n_bt, 1, d), lambda i, t: (0, i, 0, 0)),
            _resident(norm_g.shape), _resident(w_in.shape), _resident(bc.shape),
            _resident(cc.shape), _resident(lam.shape), _resident(d_skip.shape),
            _resident(w_glu.shape), _resident(b_glu.shape), _resident(conv_w.shape),
            _resident(w_ps.shape), _resident(w_pc.shape), _resident(w_out.shape),
        ],
        out_specs=pl.BlockSpec((n_bt, t_n, d), lambda i, t: (i, t, 0)),
        out_shape=jax.ShapeDtypeStruct(x.shape, F32),
        scratch_shapes=[
            pltpu.VMEM((m, d), BF16),
            pltpu.VMEM((m, d_ssm), F32),
            pltpu.VMEM((d_ssm // LANES, m, LANES), F32),
            pltpu.VMEM((m, n_state_cols), F32),
            pltpu.VMEM((m, n_state_cols), BF16),
            pltpu.VMEM((d_ssm // LANES, m, LANES), F32),
            pltpu.VMEM((m, d_ssm), F32),
            pltpu.VMEM((2, n_bt, n_state_cols // 2), F32),
            pltpu.VMEM((n_bt, t_n + SUBLANES, d_conv), F32),
            pltpu.VMEM((n_slice, m, MXU_COLS), F32),
        ],
        compiler_params=pltpu.CompilerParams(
            dimension_semantics=("arbitrary", "arbitrary"), vmem_limit_bytes=vmem),
        name="mixer",
    )(x, mod, norm_g, w_in, bc, cc, lam, d_skip, w_glu, b_glu, conv_w, w_ps, w_pc, w_out)


def _mlp_kernel(x_ref, mod_ref, g_ref, w1_ref, w2_ref, fg_ref, o_ref,
                *, slab_rows, ff_chunk, final_norm):
    d_ff = w1_ref.shape[1]
    chunks = list(range(0, d_ff, ff_chunk))
    slabs = list(range(0, x_ref.shape[1], slab_rows))
    h, f, act = {}, {}, {}

    half = slab_rows // 2

    gain = g_ref[...] * (1.0 + mod_ref[1, 0])

    def normed(r0, n_rows):
        return (_rms_normalize(x_ref[0, r0:r0 + n_rows, :]) * gain + mod_ref[0, 0]).astype(BF16)

    def up(r0, c0):
        a = jnp.maximum(_dot(h[r0], w1_ref[:, c0:c0 + ff_chunk]), 0.0)
        act[r0, c0] = (a * a).astype(BF16)

    def down(r0, c0, rows=slice(None)):
        part = _dot(act[r0, c0][rows, :], w2_ref[c0:c0 + ff_chunk, :])
        return part if r0 not in f else f[r0][rows, :] + part

    def finish(r0, f_rows, row_lo, n_rows):
        x2 = x_ref[0, r0 + row_lo:r0 + row_lo + n_rows, :] + mod_ref[2, 0] * f_rows
        o_ref[0, r0 + row_lo:r0 + row_lo + n_rows, :] = (
            _rms_normalize(x2) * fg_ref[...] if final_norm else x2)

    steps = [(r0, c0) for r0 in slabs for c0 in chunks]
    first = slabs[0]
    h_top = normed(first, half)
    a_top = jnp.maximum(_dot(h_top, w1_ref[:, 0:ff_chunk]), 0.0)
    h_bot = normed(first + half, half)
    a_bot = jnp.maximum(_dot(h_bot, w1_ref[:, 0:ff_chunk]), 0.0)
    h[first] = jnp.concatenate([h_top, h_bot], axis=0)
    act[steps[0]] = jnp.concatenate([(a_top * a_top).astype(BF16), (a_bot * a_bot).astype(BF16)],
                                    axis=0)
    for r0 in slabs[1:]:
        h[r0] = normed(r0, slab_rows)
    for prev, cur in zip(steps, steps[1:]):
        up(*cur)
        f[prev[0]] = down(*prev)
        if prev[1] == chunks[-1]:
            finish(prev[0], f[prev[0]], 0, slab_rows)
    last = steps[-1]
    for row_lo in (0, half):
        finish(last[0], down(*last, rows=slice(row_lo, row_lo + half)), row_lo, half)


def _mlp(x, mod, norm_g, w1, w2, final_g, final_norm):
    n_b, n_len, d = x.shape
    _, _, slab_rows, n_slabs, ff_chunk, _, vmem = _tiles(d)
    rows = slab_rows * n_slabs
    assert n_len % rows == 0 and w1.shape[1] % ff_chunk == 0 and slab_rows % (2 * BF16_ROWS) == 0
    return pl.pallas_call(
        functools.partial(_mlp_kernel, slab_rows=slab_rows, ff_chunk=ff_chunk,
                          final_norm=final_norm),
        grid=(n_b, n_len // rows),
        in_specs=[
            pl.BlockSpec((1, rows, d), lambda b, i: (b, i, 0)),
            pl.BlockSpec((3, 1, 1, d), lambda b, i: (1, b, 0, 0)),
            _resident(norm_g.shape), _resident(w1.shape), _resident(w2.shape),
            _resident(final_g.shape),
        ],
        out_specs=pl.BlockSpec((1, rows, d), lambda b, i: (b, i, 0)),
        out_shape=jax.ShapeDtypeStruct(x.shape, F32),
        compiler_params=pltpu.CompilerParams(
            dimension_semantics=("arbitrary", "arbitrary"), vmem_limit_bytes=vmem),
        name="mlp",
    )(x, mod, norm_g, w1, w2, final_g)


def _s5_operators(lam_re, lam_im, log_dt, b_re, b_im, c_re, c_im):
    n_g, n_p = lam_re.shape
    n_h = b_re.shape[-1]
    dt = jnp.exp(log_dt)[:, None]
    mag = jnp.exp(lam_re * dt)
    lb_re = mag * jnp.cos(lam_im * dt)
    lb_im = mag * jnp.sin(lam_im * dt)
    den = lam_re * lam_re + lam_im * lam_im
    k_re = ((lb_re - 1.0) * lam_re + lb_im * lam_im) / den
    k_im = (lb_im * lam_re - (lb_re - 1.0) * lam_im) / den
    bb_re = k_re[..., None] * b_re - k_im[..., None] * b_im
    bb_im = k_re[..., None] * b_im + k_im[..., None] * b_re

    g_half = n_g // 2
    n_pairs = g_half // 2
    n_rows = g_half * n_h
    n_cols = g_half * 2 * n_p
    b_tab = jnp.stack([bb_re, bb_im]).reshape(2, 2, n_pairs, 2, n_p, n_h)
    b_tab = b_tab.transpose(1, 5, 2, 0, 3, 4).reshape(2, n_h, n_cols)
    c_tab = jnp.stack([c_re, -c_im]).reshape(2, 2, n_pairs, 2, n_h, n_p)
    c_tab = c_tab.transpose(1, 4, 2, 0, 3, 5).reshape(2, n_h, n_cols)
    row_group = lax.broadcasted_iota(jnp.int32, (n_rows, n_cols), 0) // n_h
    col = lax.broadcasted_iota(jnp.int32, (n_rows, n_cols), 1)
    col_group = (col // (4 * n_p)) * 2 + (col // n_p) % 2
    on_diagonal = row_group == col_group

    def block_diagonal(tab):
        full = jnp.broadcast_to(tab[:, None], (2, g_half, n_h, n_cols)).reshape(2, n_rows, n_cols)
        return jnp.where(on_diagonal[None], full, 0.0).astype(BF16)

    bc = block_diagonal(b_tab)
    cc = jnp.swapaxes(block_diagonal(c_tab), 1, 2)
    lam = jnp.stack([lb_re.reshape(-1), lb_im.reshape(-1)])
    lam = jnp.broadcast_to(lam[:, None, :], (2, SUBLANES, n_g * n_p))
    return lam, bc, cc


def kernel(x, c, norm1_g, norm2_g, w_ada, b_ada, w_in, lam_re, lam_im, log_dt, b_re, b_im,
           c_re, c_im, d_skip, w_glu, b_glu, conv_w, w_proj_ssm, w_proj_conv, w_out,
           w_ff1, w_ff2, final_g):
    depth = w_in.shape[0]
    n_b, _, d = x.shape
    for l in range(depth):
        mod = _adaln(c, w_ada[l], b_ada[l]).reshape(N_MOD, n_b, 1, d)
        lam, bc, cc = _s5_operators(lam_re[l], lam_im[l], log_dt[l], b_re[l], b_im[l],
                                    c_re[l], c_im[l])
        x = _mixer(x, mod, norm1_g[l][None], w_in[l].astype(BF16), bc, cc, lam,
                   d_skip[l][None], w_glu[l].astype(BF16), b_glu[l][None], conv_w[l],
                   w_proj_ssm[l].astype(BF16), w_proj_conv[l].astype(BF16),
                   w_out[l].astype(BF16))
        x = _mlp(x, mod, norm2_g[l][None], w_ff1[l].astype(BF16), w_ff2[l].astype(BF16),
                 final_g[None], final_norm=(l == depth - 1))
    return x
```

```python
import functools

import jax
import jax.numpy as jnp
from jax import lax
from jax.experimental import pallas as pl
from jax.experimental.pallas import tpu as pltpu

SUBLANES = 8
BF16_ROWS = 16
LANES = 128
MXU_COLS = 256
VMEM_BYTES_V7X = 64 * 1024 * 1024

RMS_EPS = 1e-6
N_MOD = 6
CONV_WIDTH = 3

F32 = jnp.float32
BF16 = jnp.bfloat16


def _tiles(d_model):
    batch_tile = SUBLANES
    time_tile = 64
    mlp_rows = 512
    mlp_slabs = 2
    ff_chunk = 1024
    mixer_vmem = VMEM_BYTES_V7X - 8 * 1024 * 1024
    mlp_vmem = VMEM_BYTES_V7X - 8 * 1024 * 1024
    return batch_tile, time_tile, mlp_rows, mlp_slabs, ff_chunk, mixer_vmem, mlp_vmem


def _rms_normalize(x):
    return x * lax.rsqrt(jnp.mean(x * x, axis=-1, keepdims=True) + RMS_EPS)


def _dot(a, b):
    return jnp.dot(a, b, preferred_element_type=F32)


def _adaln_kernel(c_ref, w_ref, b_ref, o_ref):
    c = c_ref[...]
    c_act = c * jax.nn.sigmoid(c)
    o_ref[0] = _dot(c_act.astype(BF16), w_ref[...].astype(BF16)) + b_ref[...]


def _adaln(c, w_ada, b_ada):
    n_b, d = c.shape
    return pl.pallas_call(
        _adaln_kernel,
        grid=(N_MOD,),
        in_specs=[
            pl.BlockSpec((n_b, d), lambda j: (0, 0)),
            pl.BlockSpec((d, d), lambda j: (0, j)),
            pl.BlockSpec((1, d), lambda j: (0, j)),
        ],
        out_specs=pl.BlockSpec((1, n_b, d), lambda j: (j, 0, 0)),
        out_shape=jax.ShapeDtypeStruct((N_MOD, n_b, d), F32),
        name="adaln",
    )(c, w_ada, b_ada.reshape(1, N_MOD * d))


def _mixer_kernel(x_ref, mod_ref, g_ref, w_in_ref, bc_ref, cc_ref, lam_ref, dskip_ref,
                  wglu_ref, bglu_ref, convw_ref, wps_ref, wpc_ref, wout_ref,
                  o_ref,
                  h_ref, u_ref, utb_ref, bu_ref, xs_ref, ytb_ref, ybt_ref, state_ref, vpad_ref, p_ref,
                  *, time_tile, d_ssm, d_conv, d_model):
    t_n = time_tile
    n_bt = x_ref.shape[0]
    m = n_bt * t_n
    n_slab = d_ssm // LANES
    slabs_per_half = n_slab // 2
    half_cols = bu_ref.shape[1] // 2
    n_pair = bu_ref.shape[1] // (2 * LANES)
    w_p = p_ref.shape[2]
    row_halves = [slice(0, m // 2), slice(m // 2, m)]
    batch_halves = [range(0, n_bt // 2), range(n_bt // 2, n_bt)]

    @pl.when(pl.program_id(1) == 0)
    def _():
        state_ref[...] = jnp.zeros_like(state_ref)
        vpad_ref[:, 0:SUBLANES, :] = jnp.zeros((n_bt, SUBLANES, d_conv), F32)


    def project(first_col, n_cols):
        for k in range(first_col // w_p, (first_col + n_cols) // w_p):
            c0 = d_ssm + k * w_p
            p_ref[k] = _dot(h_ref[...], w_in_ref[:, c0:c0 + w_p])

    def p_cols(first_col, n_cols, rows=slice(None)):
        return jnp.concatenate(
            [p_ref[k, rows, :] for k in range(first_col // w_p, (first_col + n_cols) // w_p)],
            axis=1)

    for rows, batches in zip(row_halves, batch_halves):
        for b in batches:
            gain = g_ref[...] * (1.0 + mod_ref[1, b])
            hb = _rms_normalize(x_ref[b]) * gain + mod_ref[0, b]
            h_ref[b * t_n:(b + 1) * t_n, :] = hb.astype(BF16)
        u_ref[rows, :] = _dot(h_ref[rows, :], w_in_ref[:, 0:d_ssm])

    c_gate = 3 * d_conv
    n_early_gate = 4 * w_p
    project(0, d_conv)
    project(c_gate, n_early_gate)

    for b in range(n_bt):
        for j in range(n_slab):
            utb_ref[j, pl.ds(b, t_n, stride=n_bt), :] = (
                u_ref[b * t_n:(b + 1) * t_n, j * LANES:(j + 1) * LANES])
    for half in range(2):
        lhs = jnp.concatenate(
            [utb_ref[half * slabs_per_half + j] for j in range(slabs_per_half)], axis=1)
        bu_ref[:, half * half_cols:(half + 1) * half_cols] = _dot(lhs.astype(BF16), bc_ref[half])

    project(d_conv, 2 * d_conv)
    v3 = (p_cols(d_conv, d_conv) * p_cols(2 * d_conv, d_conv)).reshape(n_bt, t_n, d_conv)
    vpad_ref[:, SUBLANES:SUBLANES + t_n, :] = v3
    conv = convw_ref[CONV_WIDTH - 1:CONV_WIDTH, :] * v3
    for k in range(1, CONV_WIDTH):
        shifted = vpad_ref[:, SUBLANES - k:SUBLANES - k + t_n, :]
        conv = conv + convw_ref[CONV_WIDTH - 1 - k:CONV_WIDTH - k, :] * shifted
    vpad_ref[:, 0:SUBLANES, :] = vpad_ref[:, t_n:t_n + SUBLANES, :]

    yc = (p_cols(0, d_conv) * conv.reshape(m, d_conv)).astype(BF16)

    pair_cols = [p * LANES for p in range(n_pair)]
    x_re = [state_ref[0, :, c:c + LANES] for c in pair_cols]
    x_im = [state_ref[1, :, c:c + LANES] for c in pair_cols]
    steps_per_tile = BF16_ROWS // n_bt
    for t0 in range(0, t_n, steps_per_tile):
        tile_rows = slice(t0 * n_bt, (t0 + steps_per_tile) * n_bt)
        for p, c in enumerate(pair_cols):
            c_re, c_im = 2 * c, 2 * c + LANES
            new_re, new_im = [], []
            for t in range(t0, t0 + steps_per_tile):
                rows = slice(t * n_bt, (t + 1) * n_bt)
                l_re = lam_ref[0, :, c:c + LANES]
                l_im = lam_ref[1, :, c:c + LANES]
                n_re = l_re * x_re[p] - l_im * x_im[p] + bu_ref[rows, c_re:c_re + LANES]
                n_im = l_re * x_im[p] + l_im * x_re[p] + bu_ref[rows, c_im:c_im + LANES]
                x_re[p], x_im[p] = n_re, n_im
                new_re.append(n_re)
                new_im.append(n_im)
            xs_ref[tile_rows, c_re:c_re + LANES] = jnp.concatenate(new_re, axis=0).astype(BF16)
            xs_ref[tile_rows, c_im:c_im + LANES] = jnp.concatenate(new_im, axis=0).astype(BF16)
    for p, c in enumerate(pair_cols):
        state_ref[0, :, c:c + LANES] = x_re[p]
        state_ref[1, :, c:c + LANES] = x_im[p]

    for half in range(2):
        yh = _dot(xs_ref[:, half * half_cols:(half + 1) * half_cols], cc_ref[half])
        for j in range(slabs_per_half):
            ytb_ref[half * slabs_per_half + j] = yh[:, j * LANES:(j + 1) * LANES]
        if half == 0:
            project(c_gate + n_early_gate, 2 * d_model - n_early_gate - 2 * w_p)

    project(c_gate + 2 * d_model - 2 * w_p, w_p)
    conv_out = _dot(yc, wpc_ref[...])

    for b in range(n_bt):
        for j in range(n_slab):
            ybt_ref[b * t_n:(b + 1) * t_n, j * LANES:(j + 1) * LANES] = (
                ytb_ref[j, pl.ds(b, t_n, stride=n_bt), :])
    ys = ybt_ref[...] + dskip_ref[...] * u_ref[...]
    ys = jax.nn.gelu(ys)
    glu = _dot(ys.astype(BF16), wglu_ref[...])

    project(c_gate + 2 * d_model - w_p, w_p)

    ys = (ys * jax.nn.sigmoid(glu + bglu_ref[...])).astype(BF16)

    merged = []
    for rows in row_halves:
        ssm_out = _dot(ys[rows, :], wps_ref[...])
        merged.append(jax.nn.sigmoid(p_cols(c_gate, d_model, rows)) * ssm_out
                      + jax.nn.sigmoid(p_cols(c_gate + d_model, d_model, rows)) * conv_out[rows, :])
    for rows, batches, mg in zip(row_halves, batch_halves, merged):
        mixed = _dot(mg.astype(BF16), wout_ref[...])
        r0 = rows.start
        for b in batches:
            o_ref[b] = x_ref[b] + mod_ref[2, b] * mixed[b * t_n - r0:(b + 1) * t_n - r0, :]


def _resident(shape):
    zeros = (0,) * len(shape)
    return pl.BlockSpec(shape, lambda *_: zeros, pipeline_mode=pl.Buffered(1))


def _mixer(x, mod, norm_g, w_in, bc, cc, lam, d_skip, w_glu, b_glu, conv_w, w_ps, w_pc, w_out):
    n_b, n_len, d = x.shape
    d_ssm = w_glu.shape[0]
    d_conv = conv_w.shape[1]
    n_state_cols = lam.shape[2] * 2
    n_bt, t_n, _, _, _, vmem, _ = _tiles(d)
    m = n_bt * t_n
    n_slice = (w_in.shape[1] - d_ssm) // MXU_COLS
    assert n_b % n_bt == 0 and n_len % t_n == 0 and t_n % BF16_ROWS == 0
    assert d_ssm % (2 * MXU_COLS) == 0 and d_conv % MXU_COLS == 0 and d % MXU_COLS == 0
    assert w_in.shape[1] == d_ssm + 3 * d_conv + 2 * d and n_state_cols % (4 * LANES) == 0
    kern = functools.partial(_mixer_kernel, time_tile=t_n, d_ssm=d_ssm, d_conv=d_conv, d_model=d)
    return pl.pallas_call(
        kern,
        grid=(n_b // n_bt, n_len // t_n),
        in_specs=[
            pl.BlockSpec((n_bt, t_n, d), lambda i, t: (i, t, 0)),
            pl.BlockSpec((3, n_bt, 1, d), lambda i, t: (0, i, 0, 0)),
            _resident(norm_g.shape), _resident(w_in.shape), _resident(bc.shape),
            _resident(cc.shape), _resident(lam.shape), _resident(d_skip.shape),
            _resident(w_glu.shape), _resident(b_glu.shape), _resident(conv_w.shape),
            _resident(w_ps.shape), _resident(w_pc.shape), _resident(w_out.shape),
        ],
        out_specs=pl.BlockSpec((n_bt, t_n, d), lambda i, t: (i, t, 0)),
        out_shape=jax.ShapeDtypeStruct(x.shape, F32),
        scratch_shapes=[
            pltpu.VMEM((m, d), BF16),
            pltpu.VMEM((m, d_ssm), F32),
            pltpu.VMEM((d_ssm // LANES, m, LANES), F32),
            pltpu.VMEM((m, n_state_cols), F32),
            pltpu.VMEM((m, n_state_cols), BF16),
            pltpu.VMEM((d_ssm // LANES, m, LANES), F32),
            pltpu.VMEM((m, d_ssm), F32),
            pltpu.VMEM((2, n_bt, n_state_cols // 2), F32),
            pltpu.VMEM((n_bt, t_n + SUBLANES, d_conv), F32),
            pltpu.VMEM((n_slice, m, MXU_COLS), F32),
        ],
        compiler_params=pltpu.CompilerParams(
            dimension_semantics=("arbitrary", "arbitrary"), vmem_limit_bytes=vmem),
        name="mixer",
    )(x, mod, norm_g, w_in, bc, cc, lam, d_skip, w_glu, b_glu, conv_w, w_ps, w_pc, w_out)


def _mlp_kernel(x_ref, mod_ref, g_ref, w1_ref, w2_ref, fg_ref, o_ref,
                *, slab_rows, ff_chunk, final_norm):
    d_ff = w1_ref.shape[1]
    chunks = list(range(0, d_ff, ff_chunk))
    slabs = list(range(0, x_ref.shape[1], slab_rows))
    h, f, act = {}, {}, {}

    half = slab_rows // 2

    gain = g_ref[...] * (1.0 + mod_ref[1, 0])

    def normed(r0, n_rows):
        return (_rms_normalize(x_ref[0, r0:r0 + n_rows, :]) * gain + mod_ref[0, 0]).astype(BF16)

    def up(r0, c0):
        a = jnp.maximum(_dot(h[r0], w1_ref[:, c0:c0 + ff_chunk]), 0.0)
        act[r0, c0] = (a * a).astype(BF16)

    def down(r0, c0, rows=slice(None)):
        part = _dot(act[r0, c0][rows, :], w2_ref[c0:c0 + ff_chunk, :])
        return part if r0 not in f else f[r0][rows, :] + part

    def finish(r0, f_rows, row_lo, n_rows):
        x2 = x_ref[0, r0 + row_lo:r0 + row_lo + n_rows, :] + mod_ref[2, 0] * f_rows
        o_ref[0, r0 + row_lo:r0 + row_lo + n_rows, :] = (
            _rms_normalize(x2) * fg_ref[...] if final_norm else x2)

    steps = [(r0, c0) for r0 in slabs for c0 in chunks]
    first = slabs[0]
    h_top = normed(first, half)
    a_top = jnp.maximum(_dot(h_top, w1_ref[:, 0:ff_chunk]), 0.0)
    h_bot = normed(first + half, half)
    a_bot = jnp.maximum(_dot(h_bot, w1_ref[:, 0:ff_chunk]), 0.0)
    h[first] = jnp.concatenate([h_top, h_bot], axis=0)
    act[steps[0]] = jnp.concatenate([(a_top * a_top).astype(BF16), (a_bot * a_bot).astype(BF16)],
                                    axis=0)
    for r0 in slabs[1:]:
        h[r0] = normed(r0, slab_rows)
    for prev, cur in zip(steps, steps[1:]):
        up(*cur)
        f[prev[0]] = down(*prev)
        if prev[1] == chunks[-1]:
            finish(prev[0], f[prev[0]], 0, slab_rows)
    last = steps[-1]
    for row_lo in (0, half):
        finish(last[0], down(*last, rows=slice(row_lo, row_lo + half)), row_lo, half)


def _mlp(x, mod, norm_g, w1, w2, final_g, final_norm):
    n_b, n_len, d = x.shape
    _, _, slab_rows, n_slabs, ff_chunk, _, vmem = _tiles(d)
    rows = slab_rows * n_slabs
    assert n_len % rows == 0 and w1.shape[1] % ff_chunk == 0 and slab_rows % (2 * BF16_ROWS) == 0
    return pl.pallas_call(
        functools.partial(_mlp_kernel, slab_rows=slab_rows, ff_chunk=ff_chunk,
                          final_norm=final_norm),
        grid=(n_b, n_len // rows),
        in_specs=[
            pl.BlockSpec((1, rows, d), lambda b, i: (b, i, 0)),
            pl.BlockSpec((3, 1, 1, d), lambda b, i: (1, b, 0, 0)),
            _resident(norm_g.shape), _resident(w1.shape), _resident(w2.shape),
            _resident(final_g.shape),
        ],
        out_specs=pl.BlockSpec((1, rows, d), lambda b, i: (b, i, 0)),
        out_shape=jax.ShapeDtypeStruct(x.shape, F32),
        compiler_params=pltpu.CompilerParams(
            dimension_semantics=("arbitrary", "arbitrary"), vmem_limit_bytes=vmem),
        name="mlp",
    )(x, mod, norm_g, w1, w2, final_g)


def _s5_operators(lam_re, lam_im, log_dt, b_re, b_im, c_re, c_im):
    n_g, n_p = lam_re.shape
    n_h = b_re.shape[-1]
    dt = jnp.exp(log_dt)[:, None]
    mag = jnp.exp(lam_re * dt)
    lb_re = mag * jnp.cos(lam_im * dt)
    lb_im = mag * jnp.sin(lam_im * dt)
    den = lam_re * lam_re + lam_im * lam_im
    k_re = ((lb_re - 1.0) * lam_re + lb_im * lam_im) / den
    k_im = (lb_im * lam_re - (lb_re - 1.0) * lam_im) / den
    bb_re = k_re[..., None] * b_re - k_im[..., None] * b_im
    bb_im = k_re[..., None] * b_im + k_im[..., None] * b_re

    g_half = n_g // 2
    n_pairs = g_half // 2
    n_rows = g_half * n_h
    n_cols = g_half * 2 * n_p
    b_tab = jnp.stack([bb_re, bb_im]).reshape(2, 2, n_pairs, 2, n_p, n_h)
    b_tab = b_tab.transpose(1, 5, 2, 0, 3, 4).reshape(2, n_h, n_cols)
    c_tab = jnp.stack([c_re, -c_im]).reshape(2, 2, n_pairs, 2, n_h, n_p)
    c_tab = c_tab.transpose(1, 4, 2, 0, 3, 5).reshape(2, n_h, n_cols)
    row_group = lax.broadcasted_iota(jnp.int32, (n_rows, n_cols), 0) // n_h
    col = lax.broadcasted_iota(jnp.int32, (n_rows, n_cols), 1)
    col_group = (col // (4 * n_p)) * 2 + (col // n_p) % 2
    on_diagonal = row_group == col_group

    def block_diagonal(tab):
        full = jnp.broadcast_to(tab[:, None], (2, g_half, n_h, n_cols)).reshape(2, n_rows, n_cols)
        return jnp.where(on_diagonal[None], full, 0.0).astype(BF16)

    bc = block_diagonal(b_tab)
    cc = jnp.swapaxes(block_diagonal(c_tab), 1, 2)
    lam = jnp.stack([lb_re.reshape(-1), lb_im.reshape(-1)])
    lam = jnp.broadcast_to(lam[:, None, :], (2, SUBLANES, n_g * n_p))
    return lam, bc, cc


def kernel(x, c, norm1_g, norm2_g, w_ada, b_ada, w_in, lam_re, lam_im, log_dt, b_re, b_im,
           c_re, c_im, d_skip, w_glu, b_glu, conv_w, w_proj_ssm, w_proj_conv, w_out,
           w_ff1, w_ff2, final_g):
    depth = w_in.shape[0]
    n_b, _, d = x.shape
    for l in range(depth):
        mod = _adaln(c, w_ada[l], b_ada[l]).reshape(N_MOD, n_b, 1, d)
        lam, bc, cc = _s5_operators(lam_re[l], lam_im[l], log_dt[l], b_re[l], b_im[l],
                                    c_re[l], c_im[l])
        x = _mixer(x, mod, norm1_g[l][None], w_in[l].astype(BF16), bc, cc, lam,
                   d_skip[l][None], w_glu[l].astype(BF16), b_glu[l][None], conv_w[l],
                   w_proj_ssm[l].astype(BF16), w_proj_conv[l].astype(BF16),
                   w_out[l].astype(BF16))
        x = _mlp(x, mod, norm2_g[l][None], w_ff1[l].astype(BF16), w_ff2[l].astype(BF16),
                 final_g[None], final_norm=(l == depth - 1))
    return x
```
